```python
import math
import jax, jax.numpy as jnp
from jax import lax
import numpy as np

D_MODEL = 1024
BATCH = 4
SEQ = 8192
DEPTH = 4

N_MIXERS = 2
N_A_LAYERS = (DEPTH + 1) // 2
N_B_LAYERS = DEPTH // 2

A_HEADS = 16
A_HEAD_DIM = 64
A_WIDTH = A_HEADS * A_HEAD_DIM
DILATION_PAIRS = ((128, 1), (512, 4), (2048, 16))
N_DIL = len(DILATION_PAIRS)
A_QBLOCK = 128
A_IN_COLS = N_DIL * 3 * A_WIDTH + A_WIDTH

SSM_INNER = 2 * D_MODEL
SSM_HEAD_DIM = 64
SSM_HEADS = SSM_INNER // SSM_HEAD_DIM
SSM_STATE = 128
SSM_GROUPS = 4
SSM_CONV = 5
SSM_CHUNK = 128
SSM_CONV_DIM = SSM_INNER + 2 * SSM_GROUPS * SSM_STATE
SSM_IN_COLS = SSM_INNER + SSM_CONV_DIM + 2 * SSM_HEADS

DEEPNORM_ALPHA = (2 * DEPTH) ** 0.25
DEEPNORM_BETA = (8 * DEPTH) ** -0.25
LN_EPS = 1e-5
RMS_EPS = 1e-5

kernel_name = "hybrid_dilated_attn_ssd_encoder"


def _layer_norm(x, g, b):
    xf = x.astype(jnp.float32)
    mu = jnp.mean(xf, axis=-1, keepdims=True)
    var = jnp.mean(jnp.square(xf - mu), axis=-1, keepdims=True)
    y = (xf - mu) * lax.rsqrt(var + LN_EPS) * g.astype(jnp.float32) + b.astype(jnp.float32)
    return y.astype(x.dtype)


def _rms_norm(x, g):
    xf = x.astype(jnp.float32)
    y = xf * lax.rsqrt(jnp.mean(jnp.square(xf), axis=-1, keepdims=True) + RMS_EPS)
    return (y * g.astype(jnp.float32)).astype(x.dtype)


def _alibi_slopes(n):
    return jnp.asarray(2.0 ** (-8.0 * (np.arange(n, dtype=np.float32) + 1.0) / n), dtype=jnp.float32)


def _dilated_group(q, k, v, window, dil, slopes):
    bsz, s, h, dh = q.shape
    length = s // dil
    radius = window // (2 * dil)
    qb = math.gcd(length, A_QBLOCK)
    nblk = length // qb
    span = qb + 2 * radius

    def to_sub(t):
        return t.reshape(bsz, length, dil, h, dh).transpose(0, 2, 3, 1, 4)

    qs = to_sub(q).reshape(bsz, dil, h, nblk, qb, dh)
    pad = ((0, 0), (0, 0), (0, 0), (radius, radius), (0, 0))
    kp = jnp.pad(to_sub(k), pad)
    vp = jnp.pad(to_sub(v), pad)
    idx = np.arange(nblk)[:, None] * qb + np.arange(span)[None, :]
    kb = kp[:, :, :, idx]
    vb = vp[:, :, :, idx]

    scores = jnp.einsum('bdhnqe,bdhnke->bdhnqk', qs, kb).astype(jnp.float32) / math.sqrt(dh)
    delta = np.arange(span)[None, :] - radius - np.arange(qb)[:, None]
    keypos = np.arange(nblk)[:, None, None] * qb + np.arange(span)[None, None, :] - radius
    valid = (np.abs(delta)[None] <= radius) & (keypos >= 0) & (keypos < length)
    dist = jnp.asarray(np.abs(delta) * dil, dtype=jnp.float32)
    alibi = -slopes[:, None, None] * dist[None]
    scores = jnp.where(valid, scores + alibi[None, None, :, None], -jnp.inf)

    m = jnp.max(scores, axis=-1, keepdims=True)
    p = jnp.exp(scores - m)
    z = jnp.sum(p, axis=-1, keepdims=True)
    o = jnp.einsum('bdhnqk,bdhnke->bdhnqe', p.astype(v.dtype), vb) / z.astype(v.dtype)
    lse = (m + jnp.log(z))[..., 0]

    o = o.reshape(bsz, dil, h, length, dh).transpose(0, 3, 1, 2, 4).reshape(bsz, s, h, dh)
    lse = lse.reshape(bsz, dil, h, length).transpose(0, 3, 1, 2).reshape(bsz, s, h)
    return o, lse


def _dilated_attention_mixer(h, w_in, w_out):
    bsz, s, _ = h.shape
    proj = h @ w_in
    qkv = proj[..., :N_DIL * 3 * A_WIDTH].reshape(bsz, s, N_DIL, 3, A_HEADS, A_HEAD_DIM)
    gate = proj[..., N_DIL * 3 * A_WIDTH:]
    slopes = _alibi_slopes(A_HEADS)
    outs, lses = [], []
    for g, (window, dil) in enumerate(DILATION_PAIRS):
        o, l = _dilated_group(qkv[:, :, g, 0], qkv[:, :, g, 1], qkv[:, :, g, 2], window, dil, slopes)
        outs.append(o)
        lses.append(l)
    wts = jax.nn.softmax(jnp.stack(lses, axis=-1), axis=-1).astype(h.dtype)
    o = jnp.einsum('bshg,bshgd->bshd', wts, jnp.stack(outs, axis=3))
    y = o.reshape(bsz, s, A_WIDTH) * jax.nn.silu(gate)
    return y @ w_out


def _segsum_exp(a_cum):
    t = a_cum.shape[-1]
    mask = np.tril(np.ones((t, t), dtype=bool))
    diff = a_cum[..., :, None] - a_cum[..., None, :]
    return jnp.where(mask, jnp.exp(jnp.where(mask, diff, 0.0)), 0.0)


def _ssd_scan(x, dt, a, bm, cm):
    bsz, s, h, p = x.shape
    g, n = bm.shape[-2:]
    rep = h // g
    nc = s // SSM_CHUNK
    xf = x.astype(jnp.float32)
    xg = (xf * dt[..., None]).reshape(bsz, nc, SSM_CHUNK, g, rep, p)
    bc = bm.astype(jnp.float32).reshape(bsz, nc, SSM_CHUNK, g, n)
    cc = cm.astype(jnp.float32).reshape(bsz, nc, SSM_CHUNK, g, n)
    ac = (dt * a).reshape(bsz, nc, SSM_CHUNK, h).transpose(0, 3, 1, 2)
    a_cum = jnp.cumsum(ac, axis=-1)

    lmat = _segsum_exp(a_cum).reshape(bsz, g, rep, nc, SSM_CHUNK, SSM_CHUNK)
    cb = jnp.einsum('bclgn,bcsgn->bgcls', cc, bc)
    y_diag = jnp.einsum('bgrcls,bcsgrp->bclgrp', cb[:, :, None] * lmat, xg)

    decay_states = jnp.exp(a_cum[..., -1:] - a_cum).reshape(bsz, g, rep, nc, SSM_CHUNK)
    states = jnp.einsum('bcsgn,bgrcs,bcsgrp->bcgrpn', bc, decay_states, xg)

    chunk_decay = jnp.moveaxis(jnp.exp(a_cum[..., -1]).reshape(bsz, g, rep, nc), 3, 0)

    def step(hstate, inp):
        dec, st = inp
        return hstate * dec[..., None, None] + st, hstate

    h0 = jnp.zeros((bsz, g, rep, p, n), jnp.float32)
    _, h_prev = lax.scan(step, h0, (chunk_decay, jnp.moveaxis(states, 1, 0)))
    h_prev = jnp.moveaxis(h_prev, 0, 1)

    out_decay = jnp.exp(a_cum).reshape(bsz, g, rep, nc, SSM_CHUNK)
    y_off = jnp.einsum('bclgn,bcgrpn,bgrcl->bclgrp', cc, h_prev, out_decay)
    return (y_diag + y_off).reshape(bsz, s, h, p)


def _ssd_mixer(h, w_in, conv_w, conv_b, dt_bias, a_log, d_skip, norm_w, w_out):
    bsz, s, _ = h.shape
    proj = h @ w_in
    z = proj[..., :SSM_INNER]
    xbc = proj[..., SSM_INNER:SSM_INNER + SSM_CONV_DIM]
    dt_raw = proj[..., SSM_INNER + SSM_CONV_DIM:].astype(jnp.float32)

    half = SSM_CONV // 2
    xbc = lax.conv_general_dilated(
        xbc, conv_w[:, None, :].astype(xbc.dtype), window_strides=(1,), padding=[(half, half)],
        dimension_numbers=('NWC', 'WIO', 'NWC'), feature_group_count=SSM_CONV_DIM)
    xbc = jax.nn.silu(xbc + conv_b)
    xs = xbc[..., :SSM_INNER].reshape(bsz, s, SSM_HEADS, SSM_HEAD_DIM)
    bm = xbc[..., SSM_INNER:SSM_INNER + SSM_GROUPS * SSM_STATE].reshape(bsz, s, SSM_GROUPS, SSM_STATE)
    cm = xbc[..., SSM_INNER + SSM_GROUPS * SSM_STATE:].reshape(bsz, s, SSM_GROUPS, SSM_STATE)

    dtb = dt_bias.astype(jnp.float32)
    dt_f = jax.nn.softplus(dt_raw[..., :SSM_HEADS] + dtb[0])
    dt_b = jax.nn.softplus(dt_raw[..., SSM_HEADS:] + dtb[1])
    a = -jnp.exp(a_log.astype(jnp.float32))

    y_f = _ssd_scan(xs, dt_f, a[0], bm, cm)
    flip = lambda t: jnp.flip(t, axis=1)
    y_b = flip(_ssd_scan(flip(xs), flip(dt_b), a[1], flip(bm), flip(cm)))
    y = y_f + y_b + d_skip.astype(jnp.float32)[:, None] * xs.astype(jnp.float32)
    y = y.astype(h.dtype).reshape(bsz, s, SSM_INNER) * jax.nn.silu(z)
    y = _rms_norm(y, norm_w)
    return y @ w_out


def setup_inputs(seed: int = 0) -> dict:
    key = jax.random.key(seed)
    ks = jax.random.split(key, 20)
    f32 = jnp.float32
    nrm = lambda k, shape, scale: jax.random.normal(k, shape, f32) * scale

    x = jax.random.normal(ks[0], (BATCH, SEQ, D_MODEL), f32)
    c = jax.random.normal(ks[1], (BATCH, D_MODEL), f32)
    ada_w = nrm(ks[2], (DEPTH, D_MODEL, 3 * D_MODEL), D_MODEL ** -0.5)
    ada_b = nrm(ks[3], (DEPTH, 3 * D_MODEL), 0.02)
    ln_g = 1.0 + nrm(ks[4], (DEPTH, D_MODEL), 0.02)
    ln_b = nrm(ks[5], (DEPTH, D_MODEL), 0.02)

    a_w_in = nrm(ks[6], (N_A_LAYERS, D_MODEL, A_IN_COLS), D_MODEL ** -0.5)
    a_w_out = nrm(ks[7], (N_A_LAYERS, A_WIDTH, D_MODEL), DEEPNORM_BETA * A_WIDTH ** -0.5)

    b_w_in = nrm(ks[8], (N_B_LAYERS, D_MODEL, SSM_IN_COLS), D_MODEL ** -0.5)
    b_conv_w = nrm(ks[9], (N_B_LAYERS, SSM_CONV, SSM_CONV_DIM), SSM_CONV ** -0.5)
    b_conv_b = nrm(ks[10], (N_B_LAYERS, SSM_CONV_DIM), 0.02)
    dt0 = jnp.exp(jax.random.uniform(ks[11], (N_B_LAYERS, 2, SSM_HEADS), f32,
                                     math.log(1e-3), math.log(1e-1)))
    b_dt_bias = dt0 + jnp.log(-jnp.expm1(-dt0))
    b_a_log = jnp.log(jax.random.uniform(ks[12], (N_B_LAYERS, 2, SSM_HEADS), f32, 1.0, 16.0))
    b_d = 1.0 + nrm(ks[13], (N_B_LAYERS, SSM_HEADS), 0.02)
    b_norm_w = 1.0 + nrm(ks[14], (N_B_LAYERS, SSM_INNER), 0.02)
    b_w_out = nrm(ks[15], (N_B_LAYERS, SSM_INNER, D_MODEL), DEEPNORM_BETA * SSM_INNER ** -0.5)

    return {"x": x, "c": c, "ada_w": ada_w, "ada_b": ada_b, "ln_g": ln_g, "ln_b": ln_b,
            "a_w_in": a_w_in, "a_w_out": a_w_out,
            "b_w_in": b_w_in, "b_conv_w": b_conv_w, "b_conv_b": b_conv_b,
            "b_dt_bias": b_dt_bias, "b_a_log": b_a_log, "b_d": b_d,
            "b_norm_w": b_norm_w, "b_w_out": b_w_out}


def reference(x, c, ada_w, ada_b, ln_g, ln_b, a_w_in, a_w_out,
              b_w_in, b_conv_w, b_conv_b, b_dt_bias, b_a_log, b_d, b_norm_w, b_w_out):
    cond = jax.nn.silu(c)
    for i in range(DEPTH):
        mod = cond @ ada_w[i] + ada_b[i]
        shift, scale, gate = jnp.split(mod, 3, axis=-1)
        h = x * (1.0 + scale[:, None, :]) + shift[:, None, :]
        j = i // N_MIXERS
        if i % N_MIXERS == 0:
            y = _dilated_attention_mixer(h, a_w_in[j], a_w_out[j])
        else:
            y = _ssd_mixer(h, b_w_in[j], b_conv_w[j], b_conv_b[j], b_dt_bias[j],
                           b_a_log[j], b_d[j], b_norm_w[j], b_w_out[j])
        x = _layer_norm(DEEPNORM_ALPHA * x + gate[:, None, :] * y, ln_g[i], ln_b[i])
    return x
```

```python
import functools
import math

import jax
import jax.numpy as jnp
import numpy as np
from jax import lax
from jax.experimental import pallas as pl
from jax.experimental.pallas import tpu as pltpu

D_MODEL = 1024
DEPTH = 4

A_HEADS = 16
A_HEAD_DIM = 64
A_WIDTH = A_HEADS * A_HEAD_DIM
DILATION_PAIRS = ((128, 1), (512, 4), (2048, 16))
N_DIL = len(DILATION_PAIRS)
A_IN_COLS = N_DIL * 3 * A_WIDTH + A_WIDTH
A_RADIUS = 64
A_QB = 128
A_SPAN = A_QB + 2 * A_RADIUS

SSM_INNER = 2 * D_MODEL
SSM_HEAD_DIM = 64
SSM_HEADS = SSM_INNER // SSM_HEAD_DIM
SSM_STATE = 128
SSM_GROUPS = 4
SSM_CONV = 5
SSM_CHUNK = 128
SSM_CONV_DIM = SSM_INNER + 2 * SSM_GROUPS * SSM_STATE
SSM_GROUP_COLS = SSM_INNER // SSM_GROUPS
SSM_HEADS_PER_GROUP = SSM_HEADS // SSM_GROUPS

DEEPNORM_ALPHA = (2 * DEPTH) ** 0.25
LN_EPS = 1e-5
RMS_EPS = 1e-5

LANES = 128
BF16_ROWS = 16
MASKED_DIST = 1e30
VMEM_LIMIT = 48 * 1024 * 1024

F32 = jnp.float32
BF16 = jnp.bfloat16

_NT = (((1,), (1,)), ((), ()))
_TN = (((0,), (0,)), ((), ()))


def _params(*sem):
    return pltpu.CompilerParams(dimension_semantics=sem, vmem_limit_bytes=VMEM_LIMIT)


def _dot(a, b, dims=None):
    if dims is None:
        return jnp.dot(a, b, preferred_element_type=F32)
    return lax.dot_general(a, b, dims, preferred_element_type=F32)


def _split2(a):
    hi = a.astype(BF16)
    lo = (a - hi.astype(F32)).astype(BF16)
    return hi, lo


def _split3(a):
    hi = a.astype(BF16)
    r = a - hi.astype(F32)
    mid = r.astype(BF16)
    lo = (r - mid.astype(F32)).astype(BF16)
    return hi, mid, lo


def _silu(x):
    return x / (1.0 + jnp.exp(-x))


def _layer_norm_rows(r, g, b):
    mu = jnp.mean(r, axis=-1, keepdims=True)
    d = r - mu
    var = jnp.mean(d * d, axis=-1, keepdims=True)
    return d * lax.rsqrt(var + LN_EPS) * g + b


def _mod_kernel(c_ref, w_ref, b_ref, o_ref):
    cond = _silu(c_ref[...])
    w = w_ref[0]
    c_hi, c_lo = _split2(cond)
    w_hi, w_lo = _split2(w)
    acc = _dot(c_hi, w_hi) + _dot(c_lo, w_hi) + _dot(c_hi, w_lo)
    o_ref[0] = acc + b_ref[0]


def _modulation(c, ada_w, ada_b):
    bsz = c.shape[0]
    rows = 8
    cp = jnp.zeros((rows, D_MODEL), F32).at[:bsz].set(c)
    tn = 1024
    out = pl.pallas_call(
        _mod_kernel,
        grid=(DEPTH, 3 * D_MODEL // tn),
        in_specs=[
            pl.BlockSpec((rows, D_MODEL), lambda i, j: (0, 0)),
            pl.BlockSpec((1, D_MODEL, tn), lambda i, j: (i, 0, j)),
            pl.BlockSpec((1, 1, tn), lambda i, j: (i, 0, j)),
        ],
        out_specs=pl.BlockSpec((1, rows, tn), lambda i, j: (i, 0, j)),
        out_shape=jax.ShapeDtypeStruct((DEPTH, rows, 3 * D_MODEL), F32),
        compiler_params=_params("parallel", "parallel"),
        name="ada_mod",
    )(cp, ada_w, ada_b.reshape(DEPTH, 1, 3 * D_MODEL))
    return out[:, :bsz]


def _inproj_kernel(x_ref, sh_ref, sc_ref, w_ref, o_ref, h_ref):
    @pl.when(pl.program_id(2) == 0)
    def _():
        h = x_ref[0] * (1.0 + sc_ref[0]) + sh_ref[0]
        h_ref[...] = h.astype(BF16)

    o_ref[0] = _dot(h_ref[...], w_ref[...]).astype(o_ref.dtype)


def _inproj(x, shift, scale, w, tm, tn):
    bsz, s, _ = x.shape
    n = w.shape[1]
    return pl.pallas_call(
        _inproj_kernel,
        grid=(bsz, s // tm, n // tn),
        in_specs=[
            pl.BlockSpec((1, tm, D_MODEL), lambda b, i, j: (b, i, 0)),
            pl.BlockSpec((1, 1, D_MODEL), lambda b, i, j: (b, 0, 0)),
            pl.BlockSpec((1, 1, D_MODEL), lambda b, i, j: (b, 0, 0)),
            pl.BlockSpec((D_MODEL, tn), lambda b, i, j: (0, j)),
        ],
        out_specs=pl.BlockSpec((1, tm, tn), lambda b, i, j: (b, i, j)),
        out_shape=jax.ShapeDtypeStruct((bsz, s, n), BF16),
        scratch_shapes=[pltpu.VMEM((tm, D_MODEL), BF16)],
        compiler_params=_params("parallel", "parallel", "arbitrary"),
        name="inproj",
    )(x, shift, scale, w)


def _attn_kernel(q_ref, kp_ref, k_ref, kn_ref, vp_ref, v_ref, vn_ref,
                 o_ref, lse_ref, kcat, vcat, *, tq, length, dil):
    m = pl.program_id(2)
    r = A_RADIUS
    kcat[0:r] = kp_ref[0]
    kcat[r:r + tq] = k_ref[0]
    kcat[r + tq:r + tq + r] = kn_ref[0]
    vcat[0:r] = vp_ref[0]
    vcat[r:r + tq] = v_ref[0]
    vcat[r + tq:r + tq + r] = vn_ref[0]

    row = lax.broadcasted_iota(jnp.int32, (A_QB, A_SPAN), 0)
    col = lax.broadcasted_iota(jnp.int32, (A_QB, A_SPAN), 1)
    delta = col - r - row
    adist = jnp.abs(delta)
    band = adist <= r
    lane = lax.broadcasted_iota(jnp.int32, (A_QB, LANES), 1)
    lo_half = lane < A_HEAD_DIM
    ones = jnp.ones((A_SPAN, LANES), BF16)

    def sub_block(j, carry):
        r0 = pl.multiple_of(j * A_QB, A_QB)
        base = m * tq + j * A_QB - r
        pos = base + col
        valid = band & (pos >= 0) & (pos < length)
        dist = jnp.where(valid, adist.astype(F32) * float(dil), MASKED_DIST)
        lse_tile = jnp.zeros((A_QB, LANES), F32)
        for hp in range(A_HEADS // 2):
            cs = slice(hp * LANES, (hp + 1) * LANES)
            q_pair = q_ref[0, pl.ds(r0, A_QB), cs]
            k_pair = kcat[pl.ds(r0, A_SPAN), cs]
            v_ext = jnp.concatenate([vcat[pl.ds(r0, A_SPAN), cs], ones], axis=1)
            outs = []
            for a in range(2):
                h = 2 * hp + a
                slope = 2.0 ** (-8.0 * (h + 1) / A_HEADS)
                keep = lo_half if a == 0 else jnp.logical_not(lo_half)
                qm = jnp.where(keep, q_pair, jnp.zeros_like(q_pair))
                sc = _dot(qm, k_pair, _NT) * (1.0 / math.sqrt(A_HEAD_DIM)) - slope * dist
                mx = jnp.max(sc, axis=-1, keepdims=True)
                p = jnp.exp(sc - mx)
                oe = _dot(p.astype(BF16), v_ext)
                z = oe[:, LANES:]
                outs.append(oe[:, :LANES] / z)
                lse = mx + jnp.log(z)
                lse_tile = jnp.where(lane == h, lse, lse_tile)
            o_pair = jnp.where(lo_half, outs[0], outs[1])
            o_ref[0, pl.ds(r0, A_QB), cs] = o_pair.astype(o_ref.dtype)
        lse_ref[0, pl.ds(r0, A_QB), :] = lse_tile
        return carry

    lax.fori_loop(0, tq // A_QB, sub_block, 0)


def _dilated_attention(proj, g, dil, tq):
    bsz, s, cols = proj.shape
    length = s // dil
    tq = min(tq, length)
    nblk = length // tq
    cb = cols // A_WIDTH
    pv = proj.reshape(bsz, length, dil * cols)
    hb = tq // A_RADIUS
    nhalo = length // A_RADIUS

    def main_map(which):
        return lambda b, rr, m: (b, m, rr * cb + 3 * g + which)

    def prev_map(which):
        return lambda b, rr, m: (b, jnp.maximum(m * hb - 1, 0), rr * cb + 3 * g + which)

    def next_map(which):
        return lambda b, rr, m: (b, jnp.minimum((m + 1) * hb, nhalo - 1), rr * cb + 3 * g + which)

    main = lambda w: pl.BlockSpec((1, tq, A_WIDTH), main_map(w))
    prev = lambda w: pl.BlockSpec((1, A_RADIUS, A_WIDTH), prev_map(w))
    nxt = lambda w: pl.BlockSpec((1, A_RADIUS, A_WIDTH), next_map(w))

    o, lse = pl.pallas_call(
        functools.partial(_attn_kernel, tq=tq, length=length, dil=dil),
        grid=(bsz, dil, nblk),
        in_specs=[main(0), prev(1), main(1), nxt(1), prev(2), main(2), nxt(2)],
        out_specs=[
            pl.BlockSpec((1, tq, A_WIDTH), lambda b, rr, m: (b, m, rr)),
            pl.BlockSpec((1, tq, LANES), lambda b, rr, m: (b, m, rr)),
        ],
        out_shape=[
            jax.ShapeDtypeStruct((bsz, length, dil * A_WIDTH), BF16),
            jax.ShapeDtypeStruct((bsz, length, dil * LANES), F32),
        ],
        scratch_shapes=[
            pltpu.VMEM((tq + 2 * A_RADIUS, A_WIDTH), BF16),
            pltpu.VMEM((tq + 2 * A_RADIUS, A_WIDTH), BF16),
        ],
        compiler_params=_params("parallel", "parallel", "arbitrary"),
        name=f"dilated_attn_d{dil}",
    )(pv, pv, pv, pv, pv, pv, pv)
    return o.reshape(bsz, s, A_WIDTH), lse.reshape(bsz, s, LANES)


def _amerge_kernel(o1_ref, o2_ref, o3_ref, l1_ref, l2_ref, l3_ref, gate_ref, x_ref,
                   e_ref, w_ref, gm_ref, lg_ref, lb_ref, out_ref):
    l1, l2, l3 = l1_ref[0], l2_ref[0], l3_ref[0]
    lmax = jnp.maximum(jnp.maximum(l1, l2), l3)
    e1, e2, e3 = jnp.exp(l1 - lmax), jnp.exp(l2 - lmax), jnp.exp(l3 - lmax)
    inv = 1.0 / (e1 + e2 + e3)
    expand = e_ref[...]
    acc = None
    for e, o_ref in ((e1, o1_ref), (e2, o2_ref), (e3, o3_ref)):
        hi, lo = _split2(e * inv)
        wexp = _dot(hi, expand) + _dot(lo, expand)
        term = wexp * o_ref[0].astype(F32)
        acc = term if acc is None else acc + term
    y = acc * _silu(gate_ref[0].astype(F32))
    out = _dot(y.astype(BF16), w_ref[...])
    res = DEEPNORM_ALPHA * x_ref[0] + gm_ref[0] * out
    out_ref[0] = _layer_norm_rows(res, lg_ref[...], lb_ref[...])


def _amerge_out(os_, ls_, proj, x, w_out, gate_mod, ln_g, ln_b, tm):
    bsz, s, _ = x.shape
    expand = np.zeros((LANES, A_WIDTH), np.float32)
    for h in range(A_HEADS):
        expand[h, h * A_HEAD_DIM:(h + 1) * A_HEAD_DIM] = 1.0
    expand = jnp.asarray(expand, BF16)
    tok = lambda w: pl.BlockSpec((1, tm, w), lambda b, i: (b, i, 0))
    gate_block = (N_DIL * 3 * A_WIDTH) // A_WIDTH
    return pl.pallas_call(
        _amerge_kernel,
        grid=(bsz, s // tm),
        in_specs=[
            tok(A_WIDTH), tok(A_WIDTH), tok(A_WIDTH), tok(LANES), tok(LANES), tok(LANES),
            pl.BlockSpec((1, tm, A_WIDTH), lambda b, i: (b, i, gate_block)),
            tok(D_MODEL),
            pl.BlockSpec((LANES, A_WIDTH), lambda b, i: (0, 0)),
            pl.BlockSpec((A_WIDTH, D_MODEL), lambda b, i: (0, 0)),
            pl.BlockSpec((1, 1, D_MODEL), lambda b, i: (b, 0, 0)),
            pl.BlockSpec((1, D_MODEL), lambda b, i: (0, 0)),
            pl.BlockSpec((1, D_MODEL), lambda b, i: (0, 0)),
        ],
        out_specs=tok(D_MODEL),
        out_shape=jax.ShapeDtypeStruct((bsz, s, D_MODEL), F32),
        compiler_params=_params("parallel", "parallel"),
        name="attn_merge_out",
    )(*os_, *ls_, proj, x, expand, w_out, gate_mod, ln_g, ln_b)


def _dt_kernel(x_ref, sh_ref, sc_ref, w_ref, bias_ref, alog_ref,
               dt_ref, cum_ref, dtt_ref, cumt_ref, *, tm):
    h = (x_ref[0] * (1.0 + sc_ref[0]) + sh_ref[0]).astype(BF16)
    raw = _dot(h, w_ref[...]) + bias_ref[...]
    dt = jnp.maximum(raw, 0.0) + jnp.log(1.0 + jnp.exp(-jnp.abs(raw)))
    a = dt * (-jnp.exp(alog_ref[...]))
    dt_ref[0] = dt
    ri = lax.broadcasted_iota(jnp.int32, (SSM_CHUNK, SSM_CHUNK), 0)
    ci = lax.broadcasted_iota(jnp.int32, (SSM_CHUNK, SSM_CHUNK), 1)
    tril = (ci <= ri).astype(BF16)
    triu = (ci >= ri).astype(BF16)
    lane = lax.broadcasted_iota(jnp.int32, (SSM_CHUNK, LANES), 1)
    fwd_cols = lane < SSM_HEADS
    for c in range(tm // SSM_CHUNK):
        rows = slice(c * SSM_CHUNK, (c + 1) * SSM_CHUNK)
        parts = _split3(a[rows])
        cum_f = sum(_dot(tril, p) for p in parts)
        cum_b = sum(_dot(triu, p) for p in parts)
        cum = jnp.where(fwd_cols, cum_f, cum_b)
        cum_ref[0, rows, :] = cum
        dtt_ref[0, c] = dt[rows].T
        cumt_ref[0, c] = cum.T


def _dt_path(x, shift, scale, w_dt, bias, a_log, tm):
    bsz, s, _ = x.shape
    nc = s // SSM_CHUNK
    cpb = tm // SSM_CHUNK
    tok = pl.BlockSpec((1, tm, LANES), lambda b, i: (b, i, 0))
    tr = pl.BlockSpec((1, cpb, LANES, SSM_CHUNK), lambda b, i: (b, i, 0, 0))
    row = pl.BlockSpec((1, LANES), lambda b, i: (0, 0))
    return pl.pallas_call(
        functools.partial(_dt_kernel, tm=tm),
        grid=(bsz, s // tm),
        in_specs=[
            pl.BlockSpec((1, tm, D_MODEL), lambda b, i: (b, i, 0)),
            pl.BlockSpec((1, 1, D_MODEL), lambda b, i: (b, 0, 0)),
            pl.BlockSpec((1, 1, D_MODEL), lambda b, i: (b, 0, 0)),
            pl.BlockSpec((D_MODEL, LANES), lambda b, i: (0, 0)),
            row, row,
        ],
        out_specs=[tok, tok, tr, tr],
        out_shape=[
            jax.ShapeDtypeStruct((bsz, s, LANES), F32),
            jax.ShapeDtypeStruct((bsz, s, LANES), F32),
            jax.ShapeDtypeStruct((bsz, nc, LANES, SSM_CHUNK), F32),
            jax.ShapeDtypeStruct((bsz, nc, LANES, SSM_CHUNK), F32),
        ],
        compiler_params=_params("parallel", "parallel"),
        name="ssd_dt",
    )(x, shift, scale, w_dt, bias, a_log)


def _conv_kernel(p_ref, m_ref, n_ref, w_ref, b_ref, o_ref, cat, *, tm, nblk):
    i = pl.program_id(1)
    hal = BF16_ROWS
    cat[0:hal] = jnp.where(i > 0, p_ref[0].astype(F32), 0.0)
    cat[hal:hal + tm] = m_ref[0].astype(F32)
    cat[hal + tm:hal + tm + hal] = jnp.where(i < nblk - 1, n_ref[0].astype(F32), 0.0)
    half = SSM_CONV // 2
    acc = None
    for k in range(SSM_CONV):
        term = cat[hal - half + k:hal - half + k + tm, :] * w_ref[k:k + 1, :]
        acc = term if acc is None else acc + term
    o_ref[0] = _silu(acc + b_ref[...]).astype(o_ref.dtype)


def _conv_silu(zx, conv_w, conv_b, tm):
    bsz, s, _ = zx.shape
    tc = 1024
    first = SSM_INNER // tc
    nblk = s // tm
    hb = tm // BF16_ROWS
    nh = s // BF16_ROWS
    return pl.pallas_call(
        functools.partial(_conv_kernel, tm=tm, nblk=nblk),
        grid=(bsz, nblk, SSM_CONV_DIM // tc),
        in_specs=[
            pl.BlockSpec((1, BF16_ROWS, tc),
                         lambda b, i, j: (b, jnp.maximum(i * hb - 1, 0), first + j)),
            pl.BlockSpec((1, tm, tc), lambda b, i, j: (b, i, first + j)),
            pl.BlockSpec((1, BF16_ROWS, tc),
                         lambda b, i, j: (b, jnp.minimum((i + 1) * hb, nh - 1), first + j)),
            pl.BlockSpec((SSM_CONV, tc), lambda b, i, j: (0, j)),
            pl.BlockSpec((1, tc), lambda b, i, j: (0, j)),
        ],
        out_specs=pl.BlockSpec((1, tm, tc), lambda b, i, j: (b, i, j)),
        out_shape=jax.ShapeDtypeStruct((bsz, s, SSM_CONV_DIM), BF16),
        scratch_shapes=[pltpu.VMEM((tm + 2 * BF16_ROWS, tc), F32)],
        compiler_params=_params("parallel", "parallel", "parallel"),
        name="ssd_conv",
    )(zx, zx, zx, conv_w, conv_b.reshape(1, SSM_CONV_DIM))


def _ssd_kernel(*refs, reverse):
    if reverse:
        (xs_ref, b_ref, c_ref, dt_ref, cum_ref, dtt_ref, cumt_ref, e_ref,
         yf_ref, z_ref, d_ref, nw_ref, y_ref, state) = refs
    else:
        (xs_ref, b_ref, c_ref, dt_ref, cum_ref, dtt_ref, cumt_ref, e_ref,
         y_ref, state) = refs
    off = SSM_HEADS if reverse else 0
    far = 0 if reverse else SSM_CHUNK - 1

    @pl.when(pl.program_id(1) == 0)
    def _():
        state[...] = jnp.zeros_like(state)

    xs = xs_ref[0]
    dt = dt_ref[0]
    cum = cum_ref[0]
    dtt = dtt_ref[0, 0]
    cumt = cumt_ref[0, 0]
    tot = cum[far:far + 1, :]
    lane = lax.broadcasted_iota(jnp.int32, (SSM_CHUNK, LANES), 1)
    own = (lane >= off) & (lane < off + SSM_HEADS)
    w_state = dt * jnp.exp(jnp.where(own, tot - cum, 0.0))
    e_cum = jnp.exp(jnp.where(own, cum, 0.0))
    expand = e_ref[...]
    ws_hi, ws_lo = _split2(w_state)
    ec_hi, ec_lo = _split2(e_cum)
    ws_exp = _dot(ws_hi, expand) + _dot(ws_lo, expand)
    ec_exp = _dot(ec_hi, expand) + _dot(ec_lo, expand)
    xs_f = xs.astype(F32)
    xw = (xs_f * ws_exp).astype(BF16)

    ri = lax.broadcasted_iota(jnp.int32, (SSM_CHUNK, SSM_CHUNK), 0)
    ci = lax.broadcasted_iota(jnp.int32, (SSM_CHUNK, SSM_CHUNK), 1)
    mask = (ci >= ri) if reverse else (ci <= ri)

    y_groups = []
    for g in range(SSM_GROUPS):
        ns = slice(g * SSM_STATE, (g + 1) * SSM_STATE)
        gs = slice(g * SSM_GROUP_COLS, (g + 1) * SSM_GROUP_COLS)
        bg = b_ref[0, :, ns]
        cg = c_ref[0, :, ns]
        cb = _dot(cg, bg, _NT)
        st = state[g]
        y_off = _dot(cg, st.astype(BF16)) * ec_exp[:, gs]
        diag = []
        for hh in range(SSM_HEADS_PER_GROUP):
            h = g * SSM_HEADS_PER_GROUP + hh
            col = off + h
            diff = cum[:, col:col + 1] - cumt[col:col + 1, :]
            decay = jnp.exp(jnp.minimum(diff, 0.0))
            lmat = jnp.where(mask, decay * (cb * dtt[col:col + 1, :]), 0.0)
            diag.append(_dot(lmat.astype(BF16), xs[:, h * SSM_HEAD_DIM:(h + 1) * SSM_HEAD_DIM]))
        y_groups.append(jnp.concatenate(diag, axis=1) + y_off)
        state[g] = ec_exp[far:far + 1, gs] * st + _dot(bg, xw[:, gs], _TN)
    y = jnp.concatenate(y_groups, axis=1)

    if reverse:
        y = y + yf_ref[0].astype(F32) + d_ref[...] * xs_f
        y = y * _silu(z_ref[0].astype(F32))
        ms = jnp.mean(y * y, axis=-1, keepdims=True)
        y = y * lax.rsqrt(ms + RMS_EPS) * nw_ref[...]
    y_ref[0] = y.astype(y_ref.dtype)


def _ssd_scan(xbc, dt, cum, dtt, cumt, reverse, extra=None):
    bsz, s, _ = xbc.shape
    nc = s // SSM_CHUNK
    off = SSM_HEADS if reverse else 0
    expand = np.zeros((LANES, SSM_INNER), np.float32)
    for h in range(SSM_HEADS):
        expand[off + h, h * SSM_HEAD_DIM:(h + 1) * SSM_HEAD_DIM] = 1.0
    expand = jnp.asarray(expand, BF16)
    cidx = (lambda c: nc - 1 - c) if reverse else (lambda c: c)
    b_blk = SSM_INNER // (SSM_GROUPS * SSM_STATE)
    tok = lambda w, blk=0: pl.BlockSpec((1, SSM_CHUNK, w), lambda b, c: (b, cidx(c), blk))
    tr = pl.BlockSpec((1, 1, LANES, SSM_CHUNK), lambda b, c: (b, cidx(c), 0, 0))
    in_specs = [
        tok(SSM_INNER), tok(SSM_GROUPS * SSM_STATE, b_blk), tok(SSM_GROUPS * SSM_STATE, b_blk + 1),
        tok(LANES), tok(LANES), tr, tr,
        pl.BlockSpec((LANES, SSM_INNER), lambda b, c: (0, 0)),
    ]
    args = [xbc, xbc, xbc, dt, cum, dtt, cumt, expand]
    if reverse:
        yf, zx, d_exp, norm_w = extra
        row = pl.BlockSpec((1, SSM_INNER), lambda b, c: (0, 0))
        in_specs += [tok(SSM_INNER), tok(SSM_INNER), row, row]
        args += [yf, zx, d_exp, norm_w]
    return pl.pallas_call(
        functools.partial(_ssd_kernel, reverse=reverse),
        grid=(bsz, nc),
        in_specs=in_specs,
        out_specs=tok(SSM_INNER),
        out_shape=jax.ShapeDtypeStruct((bsz, s, SSM_INNER), BF16),
        scratch_shapes=[pltpu.VMEM((SSM_GROUPS, SSM_STATE, SSM_GROUP_COLS), F32)],
        compiler_params=_params("parallel", "arbitrary"),
        name="ssd_bwd" if reverse else "ssd_fwd",
    )(*args)


def _out_kernel(y_ref, x_ref, w_ref, gm_ref, lg_ref, lb_ref, out_ref):
    out = _dot(y_ref[0], w_ref[...])
    res = DEEPNORM_ALPHA * x_ref[0] + gm_ref[0] * out
    out_ref[0] = _layer_norm_rows(res, lg_ref[...], lb_ref[...])


def _out_proj(y, x, w_out, gate_mod, ln_g, ln_b, tm):
    bsz, s, k = y.shape
    return pl.pallas_call(
        _out_kernel,
        grid=(bsz, s // tm),
        in_specs=[
            pl.BlockSpec((1, tm, k), lambda b, i: (b, i, 0)),
            pl.BlockSpec((1, tm, D_MODEL), lambda b, i: (b, i, 0)),
            pl.BlockSpec((k, D_MODEL), lambda b, i: (0, 0)),
            pl.BlockSpec((1, 1, D_MODEL), lambda b, i: (b, 0, 0)),
            pl.BlockSpec((1, D_MODEL), lambda b, i: (0, 0)),
            pl.BlockSpec((1, D_MODEL), lambda b, i: (0, 0)),
        ],
        out_specs=pl.BlockSpec((1, tm, D_MODEL), lambda b, i: (b, i, 0)),
        out_shape=jax.ShapeDtypeStruct((bsz, s, D_MODEL), F32),
        compiler_params=_params("parallel", "parallel"),
        name="out_proj_ln",
    )(y, x, w_out, gate_mod, ln_g, ln_b)


def _layer_a(x, shift, scale, gate_mod, w_in, w_out, ln_g, ln_b):
    proj = _inproj(x, shift, scale, w_in.astype(BF16), tm=512, tn=2048)
    os_, ls_ = [], []
    for g, (_, dil) in enumerate(DILATION_PAIRS):
        o, lse = _dilated_attention(proj, g, dil, tq=512)
        os_.append(o)
        ls_.append(lse)
    return _amerge_out(os_, ls_, proj, x, w_out.astype(BF16), gate_mod, ln_g, ln_b, tm=512)


def _layer_b(x, shift, scale, gate_mod, w_in, conv_w, conv_b, dt_bias, a_log, d_skip,
             norm_w, w_out, ln_g, ln_b):
    n_main = SSM_INNER + SSM_CONV_DIM
    zx = _inproj(x, shift, scale, w_in[:, :n_main].astype(BF16), tm=512, tn=2560)
    w_dt = jnp.zeros((D_MODEL, LANES), BF16).at[:, :2 * SSM_HEADS].set(
        w_in[:, n_main:].astype(BF16))
    pad = lambda v: jnp.zeros((1, LANES), F32).at[0, :2 * SSM_HEADS].set(v.reshape(-1))
    dt, cum, dtt, cumt = _dt_path(x, shift, scale, w_dt, pad(dt_bias), pad(a_log), tm=512)
    xbc = _conv_silu(zx, conv_w, conv_b, tm=512)
    yf = _ssd_scan(xbc, dt, cum, dtt, cumt, reverse=False)
    d_exp = jnp.repeat(d_skip.astype(F32), SSM_HEAD_DIM).reshape(1, SSM_INNER)
    yn = _ssd_scan(xbc, dt, cum, dtt, cumt, reverse=True,
                   extra=(yf, zx, d_exp, norm_w.reshape(1, SSM_INNER).astype(F32)))
    return _out_proj(yn, x, w_out.astype(BF16), gate_mod, ln_g, ln_b, tm=512)


def kernel(x, c, ada_w, ada_b, ln_g, ln_b, a_w_in, a_w_out, b_w_in, b_conv_w, b_conv_b,
           b_dt_bias, b_a_log, b_d, b_norm_w, b_w_out):
    bsz = x.shape[0]
    mod = _modulation(c, ada_w, ada_b)
    for i in range(DEPTH):
        m3 = mod[i].reshape(bsz, 3, 1, D_MODEL)
        shift, scale, gate_mod = m3[:, 0], m3[:, 1], m3[:, 2]
        lg = ln_g[i].reshape(1, D_MODEL)
        lb = ln_b[i].reshape(1, D_MODEL)
        j = i // 2
        if i % 2 == 0:
            x = _layer_a(x, shift, scale, gate_mod, a_w_in[j], a_w_out[j], lg, lb)
        else:
            x = _layer_b(x, shift, scale, gate_mod, b_w_in[j], b_conv_w[j], b_conv_b[j],
                         b_dt_bias[j], b_a_log[j], b_d[j], b_norm_w[j], b_w_out[j], lg, lb)
    return x
```

```python
import functools
import math

import jax
import jax.numpy as jnp
import numpy as np
from jax import lax
from jax.experimental import pallas as pl
from jax.experimental.pallas import tpu as pltpu

D_MODEL = 1024
DEPTH = 4

A_HEADS = 16
A_HEAD_DIM = 64
A_WIDTH = A_HEADS * A_HEAD_DIM
DILATION_PAIRS = ((128, 1), (512, 4), (2048, 16))
N_DIL = len(DILATION_PAIRS)
A_IN_COLS = N_DIL * 3 * A_WIDTH + A_WIDTH
A_RADIUS = 64
A_QB = 128
A_SPAN = A_QB + 2 * A_RADIUS

SSM_INNER = 2 * D_MODEL
SSM_HEAD_DIM = 64
SSM_HEADS = SSM_INNER // SSM_HEAD_DIM
SSM_STATE = 128
SSM_GROUPS = 4
SSM_CONV = 5
SSM_CHUNK = 128
SSM_CONV_DIM = SSM_INNER + 2 * SSM_GROUPS * SSM_STATE
SSM_GROUP_COLS = SSM_INNER // SSM_GROUPS
SSM_HEADS_PER_GROUP = SSM_HEADS // SSM_GROUPS

DEEPNORM_ALPHA = (2 * DEPTH) ** 0.25
LN_EPS = 1e-5
RMS_EPS = 1e-5

LANES = 128
BF16_ROWS = 16
MASKED_DIST = 1e30
VMEM_LIMIT = 48 * 1024 * 1024

F32 = jnp.float32
BF16 = jnp.bfloat16

_NT = (((1,), (1,)), ((), ()))
_TN = (((0,), (0,)), ((), ()))


def _params(*sem):
    return pltpu.CompilerParams(dimension_semantics=sem, vmem_limit_bytes=VMEM_LIMIT)


def _dot(a, b, dims=None):
    if dims is None:
        return jnp.dot(a, b, preferred_element_type=F32)
    return lax.dot_general(a, b, dims, preferred_element_type=F32)


def _split2(a):
    hi = a.astype(BF16)
    lo = (a - hi.astype(F32)).astype(BF16)
    return hi, lo


def _split3(a):
    hi = a.astype(BF16)
    r = a - hi.astype(F32)
    mid = r.astype(BF16)
    lo = (r - mid.astype(F32)).astype(BF16)
    return hi, mid, lo


def _silu(x):
    return x / (1.0 + jnp.exp(-x))


def _layer_norm_rows(r, g, b):
    mu = jnp.mean(r, axis=-1, keepdims=True)
    d = r - mu
    var = jnp.mean(d * d, axis=-1, keepdims=True)
    return d * lax.rsqrt(var + LN_EPS) * g + b


def _mod_kernel(c_ref, w_ref, b_ref, o_ref):
    cond = _silu(c_ref[...])
    w = w_ref[0]
    c_hi, c_lo = _split2(cond)
    w_hi, w_lo = _split2(w)
    acc = _dot(c_hi, w_hi) + _dot(c_lo, w_hi) + _dot(c_hi, w_lo)
    o_ref[0] = acc + b_ref[0]


def _modulation(c, ada_w, ada_b):
    bsz = c.shape[0]
    rows = 8
    cp = jnp.zeros((rows, D_MODEL), F32).at[:bsz].set(c)
    tn = 1024
    out = pl.pallas_call(
        _mod_kernel,
        grid=(DEPTH, 3 * D_MODEL // tn),
        in_specs=[
            pl.BlockSpec((rows, D_MODEL), lambda i, j: (0, 0)),
            pl.BlockSpec((1, D_MODEL, tn), lambda i, j: (i, 0, j)),
            pl.BlockSpec((1, 1, tn), lambda i, j: (i, 0, j)),
        ],
        out_specs=pl.BlockSpec((1, rows, tn), lambda i, j: (i, 0, j)),
        out_shape=jax.ShapeDtypeStruct((DEPTH, rows, 3 * D_MODEL), F32),
        compiler_params=_params("parallel", "parallel"),
        name="ada_mod",
    )(cp, ada_w, ada_b.reshape(DEPTH, 1, 3 * D_MODEL))
    return out[:, :bsz]


def _inproj_kernel(x_ref, sh_ref, sc_ref, w_ref, o_ref, h_ref, *hf_ref, dil, tm):
    n = tm // dil

    @pl.when(pl.program_id(2) == 0)
    def _():
        h = x_ref[0] * (1.0 + sc_ref[0]) + sh_ref[0]
        if dil == 1:
            h_ref[...] = h.astype(BF16)
        else:
            for cb in range(D_MODEL // LANES):
                cs = slice(cb * LANES, (cb + 1) * LANES)
                hf_ref[0][cb] = h[:, cs]
                for r in range(dil):
                    h_ref[r * n:(r + 1) * n, cs] = (
                        hf_ref[0][cb, pl.ds(r, n, stride=dil), :].astype(BF16))

    res = _dot(h_ref[...], w_ref[...])
    for r in range(dil):
        o_ref[0, r] = res[r * n:(r + 1) * n].astype(o_ref.dtype)


def _inproj(x, shift, scale, w, tm, tn, dil=1):
    bsz, s, _ = x.shape
    n = w.shape[1]
    scratch = [pltpu.VMEM((tm, D_MODEL), BF16)]
    if dil > 1:
        scratch.append(pltpu.VMEM((D_MODEL // LANES, tm, LANES), F32))
    return pl.pallas_call(
        functools.partial(_inproj_kernel, dil=dil, tm=tm),
        grid=(bsz, s // tm, n // tn),
        in_specs=[
            pl.BlockSpec((1, tm, D_MODEL), lambda b, i, j: (b, i, 0)),
            pl.BlockSpec((1, 1, D_MODEL), lambda b, i, j: (b, 0, 0)),
            pl.BlockSpec((1, 1, D_MODEL), lambda b, i, j: (b, 0, 0)),
            pl.BlockSpec((D_MODEL, tn), lambda b, i, j: (0, j)),
        ],
        out_specs=pl.BlockSpec((1, dil, tm // dil, tn), lambda b, i, j: (b, 0, i, j)),
        out_shape=jax.ShapeDtypeStruct((bsz, dil, s // dil, n), BF16),
        scratch_shapes=scratch,
        compiler_params=_params("parallel", "parallel", "arbitrary"),
        name=f"inproj_d{dil}",
    )(x, shift, scale, w)


def _attn_kernel(q_ref, kp_ref, k_ref, kn_ref, vp_ref, v_ref, vn_ref,
                 o_ref, lse_ref, kcat, vcat, *, tq, length, dil):
    m = pl.program_id(2)
    r = A_RADIUS
    kcat[0:r] = kp_ref[0, 0]
    kcat[r:r + tq] = k_ref[0, 0]
    kcat[r + tq:r + tq + r] = kn_ref[0, 0]
    vcat[0:r] = vp_ref[0, 0]
    vcat[r:r + tq] = v_ref[0, 0]
    vcat[r + tq:r + tq + r] = vn_ref[0, 0]

    row = lax.broadcasted_iota(jnp.int32, (A_QB, A_SPAN), 0)
    col = lax.broadcasted_iota(jnp.int32, (A_QB, A_SPAN), 1)
    delta = col - r - row
    adist = jnp.abs(delta)
    band = adist <= r
    lane = lax.broadcasted_iota(jnp.int32, (A_QB, LANES), 1)
    lo_half = lane < A_HEAD_DIM
    ones = jnp.ones((A_SPAN, LANES), BF16)

    def sub_block(j, carry):
        r0 = pl.multiple_of(j * A_QB, A_QB)
        base = m * tq + j * A_QB - r
        pos = base + col
        valid = band & (pos >= 0) & (pos < length)
        dist = jnp.where(valid, adist.astype(F32) * float(dil), MASKED_DIST)
        lse_tile = jnp.zeros((A_QB, LANES), F32)
        for hp in range(A_HEADS // 2):
            cs = slice(hp * LANES, (hp + 1) * LANES)
            q_pair = q_ref[0, 0, pl.ds(r0, A_QB), cs]
            k_pair = kcat[pl.ds(r0, A_SPAN), cs]
            v_ext = jnp.concatenate([vcat[pl.ds(r0, A_SPAN), cs], ones], axis=1)
            outs = []
            for a in range(2):
                h = 2 * hp + a
                slope = 2.0 ** (-8.0 * (h + 1) / A_HEADS)
                keep = lo_half if a == 0 else jnp.logical_not(lo_half)
                qm = jnp.where(keep, q_pair, jnp.zeros_like(q_pair))
                sc = _dot(qm, k_pair, _NT) * (1.0 / math.sqrt(A_HEAD_DIM)) - slope * dist
                mx = jnp.max(sc, axis=-1, keepdims=True)
                p = jnp.exp(sc - mx)
                oe = _dot(p.astype(BF16), v_ext)
                z = oe[:, LANES:]
                outs.append(oe[:, :LANES] / z)
                lse = mx + jnp.log(z)
                lse_tile = jnp.where(lane == h, lse, lse_tile)
            o_pair = jnp.where(lo_half, outs[0], outs[1])
            o_ref[0, 0, pl.ds(r0, A_QB), cs] = o_pair.astype(o_ref.dtype)
        lse_ref[0, 0, pl.ds(r0, A_QB), :] = lse_tile
        return carry

    lax.fori_loop(0, tq // A_QB, sub_block, 0)


def _dilated_attention(qkv, tq):
    bsz, dil, length, _ = qkv.shape
    tq = min(tq, length)
    nblk = length // tq
    hb = tq // A_RADIUS
    nhalo = length // A_RADIUS

    main = lambda w: pl.BlockSpec((1, 1, tq, A_WIDTH), lambda b, rr, m: (b, rr, m, w))
    prev = lambda w: pl.BlockSpec(
        (1, 1, A_RADIUS, A_WIDTH), lambda b, rr, m: (b, rr, jnp.maximum(m * hb - 1, 0), w))
    nxt = lambda w: pl.BlockSpec(
        (1, 1, A_RADIUS, A_WIDTH),
        lambda b, rr, m: (b, rr, jnp.minimum((m + 1) * hb, nhalo - 1), w))

    return pl.pallas_call(
        functools.partial(_attn_kernel, tq=tq, length=length, dil=dil),
        grid=(bsz, dil, nblk),
        in_specs=[main(0), prev(1), main(1), nxt(1), prev(2), main(2), nxt(2)],
        out_specs=[
            pl.BlockSpec((1, 1, tq, A_WIDTH), lambda b, rr, m: (b, rr, m, 0)),
            pl.BlockSpec((1, 1, tq, LANES), lambda b, rr, m: (b, rr, m, 0)),
        ],
        out_shape=[
            jax.ShapeDtypeStruct((bsz, dil, length, A_WIDTH), BF16),
            jax.ShapeDtypeStruct((bsz, dil, length, LANES), F32),
        ],
        scratch_shapes=[
            pltpu.VMEM((tq + 2 * A_RADIUS, A_WIDTH), BF16),
            pltpu.VMEM((tq + 2 * A_RADIUS, A_WIDTH), BF16),
        ],
        compiler_params=_params("parallel", "parallel", "arbitrary"),
        name=f"dilated_attn_d{dil}",
    )(qkv, qkv, qkv, qkv, qkv, qkv, qkv)


def _amerge_kernel(o1_ref, o2_ref, o3_ref, l1_ref, l2_ref, l3_ref, gate_ref, x_ref,
                   e_ref, w_ref, gm_ref, lg_ref, lb_ref, out_ref, o_nat, l_nat, *, tm):
    for slot, (o_ref, l_ref) in enumerate(((o2_ref, l2_ref), (o3_ref, l3_ref))):
        dil = o_ref.shape[1]
        n = tm // dil
        for r in range(dil):
            o_r = o_ref[0, r].astype(F32)
            for cb in range(A_WIDTH // LANES):
                o_nat[slot, cb, pl.ds(r, n, stride=dil), :] = o_r[:, cb * LANES:(cb + 1) * LANES]
            l_nat[slot, pl.ds(r, n, stride=dil), :] = l_ref[0, r]
    natural = lambda slot: jnp.concatenate(
        [o_nat[slot, cb] for cb in range(A_WIDTH // LANES)], axis=1)
    l1, l2, l3 = l1_ref[0, 0], l_nat[0], l_nat[1]
    lmax = jnp.maximum(jnp.maximum(l1, l2), l3)
    e1, e2, e3 = jnp.exp(l1 - lmax), jnp.exp(l2 - lmax), jnp.exp(l3 - lmax)
    inv = 1.0 / (e1 + e2 + e3)
    expand = e_ref[...]
    acc = None
    for e, o in ((e1, o1_ref[0, 0].astype(F32)), (e2, natural(0)), (e3, natural(1))):
        hi, lo = _split2(e * inv)
        wexp = _dot(hi, expand) + _dot(lo, expand)
        term = wexp * o
        acc = term if acc is None else acc + term
    y = acc * _silu(gate_ref[0, 0].astype(F32))
    out = _dot(y.astype(BF16), w_ref[...])
    res = DEEPNORM_ALPHA * x_ref[0] + gm_ref[0] * out
    out_ref[0] = _layer_norm_rows(res, lg_ref[...], lb_ref[...])


def _amerge_out(os_, ls_, qkvg, x, w_out, gate_mod, ln_g, ln_b, tm):
    bsz, s, _ = x.shape
    expand = np.zeros((LANES, A_WIDTH), np.float32)
    for h in range(A_HEADS):
        expand[h, h * A_HEAD_DIM:(h + 1) * A_HEAD_DIM] = 1.0
    expand = jnp.asarray(expand, BF16)
    tok = lambda w: pl.BlockSpec((1, tm, w), lambda b, i: (b, i, 0))

    def grp(arr, blk=0):
        dil, w = arr.shape[1], (A_WIDTH if arr.shape[3] > LANES else LANES)
        return pl.BlockSpec((1, dil, tm // dil, w), lambda b, i: (b, 0, i, blk))

    return pl.pallas_call(
        functools.partial(_amerge_kernel, tm=tm),
        grid=(bsz, s // tm),
        in_specs=[
            grp(os_[0]), grp(os_[1]), grp(os_[2]), grp(ls_[0]), grp(ls_[1]), grp(ls_[2]),
            grp(qkvg, 3),
            tok(D_MODEL),
            pl.BlockSpec((LANES, A_WIDTH), lambda b, i: (0, 0)),
            pl.BlockSpec((A_WIDTH, D_MODEL), lambda b, i: (0, 0)),
            pl.BlockSpec((1, 1, D_MODEL), lambda b, i: (b, 0, 0)),
            pl.BlockSpec((1, D_MODEL), lambda b, i: (0, 0)),
            pl.BlockSpec((1, D_MODEL), lambda b, i: (0, 0)),
        ],
        out_specs=tok(D_MODEL),
        out_shape=jax.ShapeDtypeStruct((bsz, s, D_MODEL), F32),
        scratch_shapes=[
            pltpu.VMEM((2, A_WIDTH // LANES, tm, LANES), F32),
            pltpu.VMEM((2, tm, LANES), F32),
        ],
        compiler_params=_params("parallel", "parallel"),
        name="attn_merge_out",
    )(*os_, *ls_, qkvg, x, expand, w_out, gate_mod, ln_g, ln_b)


def _dt_kernel(x_ref, sh_ref, sc_ref, w_ref, bias_ref, alog_ref,
               dt_ref, cum_ref, dtt_ref, cumt_ref, *, tm):
    h = (x_ref[0] * (1.0 + sc_ref[0]) + sh_ref[0]).astype(BF16)
    raw = _dot(h, w_ref[...]) + bias_ref[...]
    dt = jnp.maximum(raw, 0.0) + jnp.log(1.0 + jnp.exp(-jnp.abs(raw)))
    a = dt * (-jnp.exp(alog_ref[...]))
    dt_ref[0] = dt
    ri = lax.broadcasted_iota(jnp.int32, (SSM_CHUNK, SSM_CHUNK), 0)
    ci = lax.broadcasted_iota(jnp.int32, (SSM_CHUNK, SSM_CHUNK), 1)
    tril = (ci <= ri).astype(BF16)
    triu = (ci >= ri).astype(BF16)
    lane = lax.broadcasted_iota(jnp.int32, (SSM_CHUNK, LANES), 1)
    fwd_cols = lane < SSM_HEADS
    for c in range(tm // SSM_CHUNK):
        rows = slice(c * SSM_CHUNK, (c + 1) * SSM_CHUNK)
        parts = _split3(a[rows])
        cum_f = sum(_dot(tril, p) for p in parts)
        cum_b = sum(_dot(triu, p) for p in parts)
        cum = jnp.where(fwd_cols, cum_f, cum_b)
        cum_ref[0, rows, :] = cum
        dtt_ref[0, c] = dt[rows].T
        cumt_ref[0, c] = cum.T


def _dt_path(x, shift, scale, w_dt, bias, a_log, tm):
    bsz, s, _ = x.shape
    nc = s // SSM_CHUNK
    cpb = tm // SSM_CHUNK
    tok = pl.BlockSpec((1, tm, LANES), lambda b, i: (b, i, 0))
    tr = pl.BlockSpec((1, cpb, LANES, SSM_CHUNK), lambda b, i: (b, i, 0, 0))
    row = pl.BlockSpec((1, LANES), lambda b, i: (0, 0))
    return pl.pallas_call(
        functools.partial(_dt_kernel, tm=tm),
        grid=(bsz, s // tm),
        in_specs=[
            pl.BlockSpec((1, tm, D_MODEL), lambda b, i: (b, i, 0)),
            pl.BlockSpec((1, 1, D_MODEL), lambda b, i: (b, 0, 0)),
            pl.BlockSpec((1, 1, D_MODEL), lambda b, i: (b, 0, 0)),
            pl.BlockSpec((D_MODEL, LANES), lambda b, i: (0, 0)),
            row, row,
        ],
        out_specs=[tok, tok, tr, tr],
        out_shape=[
            jax.ShapeDtypeStruct((bsz, s, LANES), F32),
            jax.ShapeDtypeStruct((bsz, s, LANES), F32),
            jax.ShapeDtypeStruct((bsz, nc, LANES, SSM_CHUNK), F32),
            jax.ShapeDtypeStruct((bsz, nc, LANES, SSM_CHUNK), F32),
        ],
        compiler_params=_params("parallel", "parallel"),
        name="ssd_dt",
    )(x, shift, scale, w_dt, bias, a_log)


def _conv_kernel(p_ref, m_ref, n_ref, w_ref, b_ref, o_ref, cat, *, tm, nblk):
    i = pl.program_id(1)
    hal = BF16_ROWS
    cat[0:hal] = jnp.where(i > 0, p_ref[0].astype(F32), 0.0)
    cat[hal:hal + tm] = m_ref[0].astype(F32)
    cat[hal + tm:hal + tm + hal] = jnp.where(i < nblk - 1, n_ref[0].astype(F32), 0.0)
    half = SSM_CONV // 2
    acc = None
    for k in range(SSM_CONV):
        term = cat[hal - half + k:hal - half + k + tm, :] * w_ref[k:k + 1, :]
        acc = term if acc is None else acc + term
    o_ref[0] = _silu(acc + b_ref[...]).astype(o_ref.dtype)


def _conv_silu(zx, conv_w, conv_b, tm):
    bsz, s, _ = zx.shape
    tc = 1024
    first = SSM_INNER // tc
    nblk = s // tm
    hb = tm // BF16_ROWS
    nh = s // BF16_ROWS
    return pl.pallas_call(
        functools.partial(_conv_kernel, tm=tm, nblk=nblk),
        grid=(bsz, nblk, SSM_CONV_DIM // tc),
        in_specs=[
            pl.BlockSpec((1, BF16_ROWS, tc),
                         lambda b, i, j: (b, jnp.maximum(i * hb - 1, 0), first + j)),
            pl.BlockSpec((1, tm, tc), lambda b, i, j: (b, i, first + j)),
            pl.BlockSpec((1, BF16_ROWS, tc),
                         lambda b, i, j: (b, jnp.minimum((i + 1) * hb, nh - 1), first + j)),
            pl.BlockSpec((SSM_CONV, tc), lambda b, i, j: (0, j)),
            pl.BlockSpec((1, tc), lambda b, i, j: (0, j)),
        ],
        out_specs=pl.BlockSpec((1, tm, tc), lambda b, i, j: (b, i, j)),
        out_shape=jax.ShapeDtypeStruct((bsz, s, SSM_CONV_DIM), BF16),
        scratch_shapes=[pltpu.VMEM((tm + 2 * BF16_ROWS, tc), F32)],
        compiler_params=_params("parallel", "parallel", "parallel"),
        name="ssd_conv",
    )(zx, zx, zx, conv_w, conv_b.reshape(1, SSM_CONV_DIM))


def _ssd_kernel(*refs, reverse):
    if reverse:
        (xs_ref, b_ref, c_ref, dt_ref, cum_ref, dtt_ref, cumt_ref, e_ref,
         yf_ref, z_ref, d_ref, nw_ref, y_ref, state) = refs
    else:
        (xs_ref, b_ref, c_ref, dt_ref, cum_ref, dtt_ref, cumt_ref, e_ref,
         y_ref, state) = refs
    off = SSM_HEADS if reverse else 0
    far = 0 if reverse else SSM_CHUNK - 1

    @pl.when(pl.program_id(1) == 0)
    def _():
        state[...] = jnp.zeros_like(state)

    xs = xs_ref[0]
    dt = dt_ref[0]
    cum = cum_ref[0]
    dtt = dtt_ref[0, 0]
    cumt = cumt_ref[0, 0]
    tot = cum[far:far + 1, :]
    lane = lax.broadcasted_iota(jnp.int32, (SSM_CHUNK, LANES), 1)
    own = (lane >= off) & (lane < off + SSM_HEADS)
    w_state = dt * jnp.exp(jnp.where(own, tot - cum, 0.0))
    e_cum = jnp.exp(jnp.where(own, cum, 0.0))
    expand = e_ref[...]
    ws_hi, ws_lo = _split2(w_state)
    ec_hi, ec_lo = _split2(e_cum)
    ws_exp = _dot(ws_hi, expand) + _dot(ws_lo, expand)
    ec_exp = _dot(ec_hi, expand) + _dot(ec_lo, expand)
    xs_f = xs.astype(F32)
    xw = (xs_f * ws_exp).astype(BF16)

    ri = lax.broadcasted_iota(jnp.int32, (SSM_CHUNK, SSM_CHUNK), 0)
    ci = lax.broadcasted_iota(jnp.int32, (SSM_CHUNK, SSM_CHUNK), 1)
    mask = (ci >= ri) if reverse else (ci <= ri)

    y_groups = []
    for g in range(SSM_GROUPS):
        ns = slice(g * SSM_STATE, (g + 1) * SSM_STATE)
        gs = slice(g * SSM_GROUP_COLS, (g + 1) * SSM_GROUP_COLS)
        bg = b_ref[0, :, ns]
        cg = c_ref[0, :, ns]
        cb = _dot(cg, bg, _NT)
        st = state[g]
        y_off = _dot(cg, st.astype(BF16)) * ec_exp[:, gs]
        diag = []
        for hh in range(SSM_HEADS_PER_GROUP):
            h = g * SSM_HEADS_PER_GROUP + hh
            col = off + h
            diff = cum[:, col:col + 1] - cumt[col:col + 1, :]
            decay = jnp.exp(jnp.minimum(diff, 0.0))
            lmat = jnp.where(mask, decay * (cb * dtt[col:col + 1, :]), 0.0)
            diag.append(_dot(lmat.astype(BF16), xs[:, h * SSM_HEAD_DIM:(h + 1) * SSM_HEAD_DIM]))
        y_groups.append(jnp.concatenate(diag, axis=1) + y_off)
        state[g] = ec_exp[far:far + 1, gs] * st + _dot(bg, xw[:, gs], _TN)
    y = jnp.concatenate(y_groups, axis=1)

    if reverse:
        y = y + yf_ref[0].astype(F32) + d_ref[...] * xs_f
        y = y * _silu(z_ref[0].astype(F32))
        ms = jnp.mean(y * y, axis=-1, keepdims=True)
        y = y * lax.rsqrt(ms + RMS_EPS) * nw_ref[...]
    y_ref[0] = y.astype(y_ref.dtype)


def _ssd_scan(xbc, dt, cum, dtt, cumt, reverse, extra=None):
    bsz, s, _ = xbc.shape
    nc = s // SSM_CHUNK
    off = SSM_HEADS if reverse else 0
    expand = np.zeros((LANES, SSM_INNER), np.float32)
    for h in range(SSM_HEADS):
        expand[off + h, h * SSM_HEAD_DIM:(h + 1) * SSM_HEAD_DIM] = 1.0
    expand = jnp.asarray(expand, BF16)
    cidx = (lambda c: nc - 1 - c) if reverse else (lambda c: c)
    b_blk = SSM_INNER // (SSM_GROUPS * SSM_STATE)
    tok = lambda w, blk=0: pl.BlockSpec((1, SSM_CHUNK, w), lambda b, c: (b, cidx(c), blk))
    tr = pl.BlockSpec((1, 1, LANES, SSM_CHUNK), lambda b, c: (b, cidx(c), 0, 0))
    in_specs = [
        tok(SSM_INNER), tok(SSM_GROUPS * SSM_STATE, b_blk), tok(SSM_GROUPS * SSM_STATE, b_blk + 1),
        tok(LANES), tok(LANES), tr, tr,
        pl.BlockSpec((LANES, SSM_INNER), lambda b, c: (0, 0)),
    ]
    args = [xbc, xbc, xbc, dt, cum, dtt, cumt, expand]
    if reverse:
        yf, zx, d_exp, norm_w = extra
        row = pl.BlockSpec((1, SSM_INNER), lambda b, c: (0, 0))
        in_specs += [tok(SSM_INNER), tok(SSM_INNER), row, row]
        args += [yf, zx, d_exp, norm_w]
    return pl.pallas_call(
        functools.partial(_ssd_kernel, reverse=reverse),
        grid=(bsz, nc),
        in_specs=in_specs,
        out_specs=tok(SSM_INNER),
        out_shape=jax.ShapeDtypeStruct((bsz, s, SSM_INNER), BF16),
        scratch_shapes=[pltpu.VMEM((SSM_GROUPS, SSM_STATE, SSM_GROUP_COLS), F32)],
        compiler_params=_params("parallel", "arbitrary"),
        name="ssd_bwd" if reverse else "ssd_fwd",
    )(*args)


def _out_kernel(y_ref, x_ref, w_ref, gm_ref, lg_ref, lb_ref, out_ref):
    out = _dot(y_ref[0], w_ref[...])
    res = DEEPNORM_ALPHA * x_ref[0] + gm_ref[0] * out
    out_ref[0] = _layer_norm_rows(res, lg_ref[...], lb_ref[...])


def _out_proj(y, x, w_out, gate_mod, ln_g, ln_b, tm):
    bsz, s, k = y.shape
    return pl.pallas_call(
        _out_kernel,
        grid=(bsz, s // tm),
        in_specs=[
            pl.BlockSpec((1, tm, k), lambda b, i: (b, i, 0)),
            pl.BlockSpec((1, tm, D_MODEL), lambda b, i: (b, i, 0)),
            pl.BlockSpec((k, D_MODEL), lambda b, i: (0, 0)),
            pl.BlockSpec((1, 1, D_MODEL), lambda b, i: (b, 0, 0)),
            pl.BlockSpec((1, D_MODEL), lambda b, i: (0, 0)),
            pl.BlockSpec((1, D_MODEL), lambda b, i: (0, 0)),
        ],
        out_specs=pl.BlockSpec((1, tm, D_MODEL), lambda b, i: (b, i, 0)),
        out_shape=jax.ShapeDtypeStruct((bsz, s, D_MODEL), F32),
        compiler_params=_params("parallel", "parallel"),
        name="out_proj_ln",
    )(y, x, w_out, gate_mod, ln_g, ln_b)


def _layer_a(x, shift, scale, gate_mod, w_in, w_out, ln_g, ln_b):
    gw = 3 * A_WIDTH
    w_bf = w_in.astype(BF16)
    os_, ls_ = [], []
    for g, (_, dil) in enumerate(DILATION_PAIRS):
        w_g = w_bf[:, g * gw:(g + 1) * gw]
        if g == 0:
            w_g = jnp.concatenate([w_g, w_bf[:, N_DIL * gw:]], axis=1)
        qkv = _inproj(x, shift, scale, w_g, tm=512, tn=2048 if g == 0 else gw, dil=dil)
        if g == 0:
            qkvg = qkv
        o, lse = _dilated_attention(qkv, tq=512)
        os_.append(o)
        ls_.append(lse)
    return _amerge_out(os_, ls_, qkvg, x, w_out.astype(BF16), gate_mod, ln_g, ln_b, tm=512)


def _layer_b(x, shift, scale, gate_mod, w_in, conv_w, conv_b, dt_bias, a_log, d_skip,
             norm_w, w_out, ln_g, ln_b):
    n_main = SSM_INNER + SSM_CONV_DIM
    zx = _inproj(x, shift, scale, w_in[:, :n_main].astype(BF16), tm=512, tn=2560)[:, 0]
    w_dt = jnp.zeros((D_MODEL, LANES), BF16).at[:, :2 * SSM_HEADS].set(
        w_in[:, n_main:].astype(BF16))
    pad = lambda v: jnp.zeros((1, LANES), F32).at[0, :2 * SSM_HEADS].set(v.reshape(-1))
    dt, cum, dtt, cumt = _dt_path(x, shift, scale, w_dt, pad(dt_bias), pad(a_log), tm=512)
    xbc = _conv_silu(zx, conv_w, conv_b, tm=512)
    yf = _ssd_scan(xbc, dt, cum, dtt, cumt, reverse=False)
    d_exp = jnp.repeat(d_skip.astype(F32), SSM_HEAD_DIM).reshape(1, SSM_INNER)
    yn = _ssd_scan(xbc, dt, cum, dtt, cumt, reverse=True,
                   extra=(yf, zx, d_exp, norm_w.reshape(1, SSM_INNER).astype(F32)))
    return _out_proj(yn, x, w_out.astype(BF16), gate_mod, ln_g, ln_b, tm=512)


def kernel(x, c, ada_w, ada_b, ln_g, ln_b, a_w_in, a_w_out, b_w_in, b_conv_w, b_conv_b,
           b_dt_bias, b_a_log, b_d, b_norm_w, b_w_out):
    bsz = x.shape[0]
    mod = _modulation(c, ada_w, ada_b)
    for i in range(DEPTH):
        m3 = mod[i].reshape(bsz, 3, 1, D_MODEL)
        shift, scale, gate_mod = m3[:, 0], m3[:, 1], m3[:, 2]
        lg = ln_g[i].reshape(1, D_MODEL)
        lb = ln_b[i].reshape(1, D_MODEL)
        j = i // 2
        if i % 2 == 0:
            x = _layer_a(x, shift, scale, gate_mod, a_w_in[j], a_w_out[j], lg, lb)
        else:
            x = _layer_b(x, shift, scale, gate_mod, b_w_in[j], b_conv_w[j], b_conv_b[j],
                         b_dt_bias[j], b_a_log[j], b_d[j], b_norm_w[j], b_w_out[j], lg, lb)
    return x
```

```python
import functools
import math

import jax
import jax.numpy as jnp
import numpy as np
from jax import lax
from jax.experimental import pallas as pl
from jax.experimental.pallas import tpu as pltpu

D_MODEL = 1024
DEPTH = 4

A_HEADS = 16
A_HEAD_DIM = 64
A_WIDTH = A_HEADS * A_HEAD_DIM
DILATION_PAIRS = ((128, 1), (512, 4), (2048, 16))
N_DIL = len(DILATION_PAIRS)
A_IN_COLS = N_DIL * 3 * A_WIDTH + A_WIDTH
A_RADIUS = 64
A_QB = 128
A_SPAN = A_QB + 2 * A_RADIUS

SSM_INNER = 2 * D_MODEL
SSM_HEAD_DIM = 64
SSM_HEADS = SSM_INNER // SSM_HEAD_DIM
SSM_STATE = 128
SSM_GROUPS = 4
SSM_CONV = 5
SSM_CHUNK = 128
SSM_CONV_DIM = SSM_INNER + 2 * SSM_GROUPS * SSM_STATE
SSM_GROUP_COLS = SSM_INNER // SSM_GROUPS
SSM_HEADS_PER_GROUP = SSM_HEADS // SSM_GROUPS

DEEPNORM_ALPHA = (2 * DEPTH) ** 0.25
LN_EPS = 1e-5
RMS_EPS = 1e-5

LANES = 128
BF16_ROWS = 16
MASKED_DIST = 1e30
LOG2E = math.log2(math.e)
VMEM_LIMIT = 48 * 1024 * 1024

F32 = jnp.float32
BF16 = jnp.bfloat16

_NT = (((1,), (1,)), ((), ()))
_TN = (((0,), (0,)), ((), ()))


def _params(*sem):
    return pltpu.CompilerParams(dimension_semantics=sem, vmem_limit_bytes=VMEM_LIMIT)


def _dot(a, b, dims=None):
    if dims is None:
        return jnp.dot(a, b, preferred_element_type=F32)
    return lax.dot_general(a, b, dims, preferred_element_type=F32)


def _split2(a):
    hi = a.astype(BF16)
    lo = (a - hi.astype(F32)).astype(BF16)
    return hi, lo


def _split3(a):
    hi = a.astype(BF16)
    r = a - hi.astype(F32)
    mid = r.astype(BF16)
    lo = (r - mid.astype(F32)).astype(BF16)
    return hi, mid, lo


def _silu(x):
    return x / (1.0 + jnp.exp(-x))


def _layer_norm_rows(r, g, b):
    mu = jnp.mean(r, axis=-1, keepdims=True)
    d = r - mu
    var = jnp.mean(d * d, axis=-1, keepdims=True)
    return d * lax.rsqrt(var + LN_EPS) * g + b


def _mod_kernel(c_ref, w_ref, b_ref, o_ref):
    cond = _silu(c_ref[...])
    w = w_ref[0]
    c_hi, c_lo = _split2(cond)
    w_hi, w_lo = _split2(w)
    acc = _dot(c_hi, w_hi) + _dot(c_lo, w_hi) + _dot(c_hi, w_lo)
    o_ref[0] = acc + b_ref[0]


def _modulation(c, ada_w, ada_b):
    bsz = c.shape[0]
    rows = 8
    cp = jnp.zeros((rows, D_MODEL), F32).at[:bsz].set(c)
    tn = 1024
    out = pl.pallas_call(
        _mod_kernel,
        grid=(DEPTH, 3 * D_MODEL // tn),
        in_specs=[
            pl.BlockSpec((rows, D_MODEL), lambda i, j: (0, 0)),
            pl.BlockSpec((1, D_MODEL, tn), lambda i, j: (i, 0, j)),
            pl.BlockSpec((1, 1, tn), lambda i, j: (i, 0, j)),
        ],
        out_specs=pl.BlockSpec((1, rows, tn), lambda i, j: (i, 0, j)),
        out_shape=jax.ShapeDtypeStruct((DEPTH, rows, 3 * D_MODEL), F32),
        compiler_params=_params("parallel", "parallel"),
        name="ada_mod",
    )(cp, ada_w, ada_b.reshape(DEPTH, 1, 3 * D_MODEL))
    return out[:, :bsz]


def _inproj_kernel(x_ref, sh_ref, sc_ref, w_ref, o_ref, h_ref, *hf_ref, dil, tm, tn):
    n = tm // dil
    h = x_ref[0] * (1.0 + sc_ref[0]) + sh_ref[0]
    if dil == 1:
        h_ref[...] = h.astype(BF16)
    else:
        for cb in range(D_MODEL // LANES):
            cs = slice(cb * LANES, (cb + 1) * LANES)
            hf_ref[0][cb] = h[:, cs]
            for r in range(dil):
                h_ref[r * n:(r + 1) * n, cs] = (
                    hf_ref[0][cb, pl.ds(r, n, stride=dil), :].astype(BF16))

    hb = h_ref[...]
    for j in range(w_ref.shape[1] // tn):
        cols = slice(j * tn, (j + 1) * tn)
        res = _dot(hb, w_ref[:, cols])
        for r in range(dil):
            o_ref[0, r, :, cols] = res[r * n:(r + 1) * n].astype(o_ref.dtype)


def _inproj(x, shift, scale, w, tm, tn, dil=1):
    bsz, s, _ = x.shape
    n = w.shape[1]
    scratch = [pltpu.VMEM((tm, D_MODEL), BF16)]
    if dil > 1:
        scratch.append(pltpu.VMEM((D_MODEL // LANES, tm, LANES), F32))
    return pl.pallas_call(
        functools.partial(_inproj_kernel, dil=dil, tm=tm, tn=tn),
        grid=(bsz, s // tm),
        in_specs=[
            pl.BlockSpec((1, tm, D_MODEL), lambda b, i: (b, i, 0)),
            pl.BlockSpec((1, 1, D_MODEL), lambda b, i: (b, 0, 0)),
            pl.BlockSpec((1, 1, D_MODEL), lambda b, i: (b, 0, 0)),
            pl.BlockSpec((D_MODEL, n), lambda b, i: (0, 0), pipeline_mode=pl.Buffered(1)),
        ],
        out_specs=pl.BlockSpec((1, dil, tm // dil, n), lambda b, i: (b, 0, i, 0)),
        out_shape=jax.ShapeDtypeStruct((bsz, dil, s // dil, n), BF16),
        scratch_shapes=scratch,
        compiler_params=_params("parallel", "parallel"),
        name=f"inproj_d{dil}",
    )(x, shift, scale, w)


def _attn_kernel(q_ref, kp_ref, k_ref, kn_ref, vp_ref, v_ref, vn_ref,
                 o_ref, lse_ref, kcat, vcat, bias, *, tq, length, dil):
    m = pl.program_id(2)
    r = A_RADIUS
    kcat[0:r] = kp_ref[0, 0]
    kcat[r:r + tq] = k_ref[0, 0]
    kcat[r + tq:r + tq + r] = kn_ref[0, 0]
    vcat[0:r] = vp_ref[0, 0]
    vcat[r:r + tq] = v_ref[0, 0]
    vcat[r + tq:r + tq + r] = vn_ref[0, 0]

    @pl.when((pl.program_id(0) == 0) & (pl.program_id(1) == 0) & (m == 0))
    def _():
        row = lax.broadcasted_iota(jnp.int32, (A_QB, A_SPAN), 0)
        col = lax.broadcasted_iota(jnp.int32, (A_QB, A_SPAN), 1)
        adist = jnp.abs(col - r - row)
        band = adist <= r
        for t, valid in enumerate((band, band & (col >= r), band & (col < A_SPAN - r))):
            dist = jnp.where(valid, adist.astype(F32) * float(dil), MASKED_DIST)
            for h in range(A_HEADS):
                bias[t * A_HEADS + h] = -(2.0 ** (-8.0 * (h + 1) / A_HEADS)) * dist

    lane = lax.broadcasted_iota(jnp.int32, (A_QB, LANES), 1)
    lo_half = lane < A_HEAD_DIM
    ones = jnp.ones((A_SPAN, LANES), BF16)

    def sub_block(j, carry):
        r0 = pl.multiple_of(j * A_QB, A_QB)
        base = m * tq + j * A_QB - r
        variant = jnp.where(base < 0, 1, jnp.where(base + A_SPAN > length, 2, 0))
        lse_tile = jnp.zeros((A_QB, LANES), F32)
        for hp in range(A_HEADS // 2):
            cs = slice(hp * LANES, (hp + 1) * LANES)
            q_pair = q_ref[0, 0, pl.ds(r0, A_QB), cs]
            k_pair = kcat[pl.ds(r0, A_SPAN), cs]
            v_ext = jnp.concatenate([vcat[pl.ds(r0, A_SPAN), cs], ones], axis=1)
            outs = []
            for a in range(2):
                h = 2 * hp + a
                keep = lo_half if a == 0 else jnp.logical_not(lo_half)
                qm = jnp.where(keep, q_pair, jnp.zeros_like(q_pair))
                sc = _dot(qm, k_pair, _NT) + bias[variant * A_HEADS + h]
                mx = jnp.max(sc, axis=-1, keepdims=True)
                p = jnp.exp(sc - mx)
                oe = _dot(p.astype(BF16), v_ext)
                z = oe[:, LANES:]
                outs.append(oe[:, :LANES] / z)
                lse = mx + jnp.log(z)
                lse_tile = jnp.where(lane == h, lse, lse_tile)
            o_pair = jnp.where(lo_half, outs[0], outs[1])
            o_ref[0, 0, pl.ds(r0, A_QB), cs] = o_pair.astype(o_ref.dtype)
        lse_ref[0, 0, pl.ds(r0, A_QB), :] = lse_tile
        return carry

    lax.fori_loop(0, tq // A_QB, sub_block, 0)


def _dilated_attention(qkv, tq):
    bsz, dil, length, _ = qkv.shape
    assert length >= 2 * A_QB, "a score tile may touch only one end of the sequence"
    tq = min(tq, length)
    nblk = length // tq
    hb = tq // A_RADIUS
    nhalo = length // A_RADIUS

    main = lambda w: pl.BlockSpec((1, 1, tq, A_WIDTH), lambda b, rr, m: (b, rr, m, w))
    prev = lambda w: pl.BlockSpec(
        (1, 1, A_RADIUS, A_WIDTH), lambda b, rr, m: (b, rr, jnp.maximum(m * hb - 1, 0), w))
    nxt = lambda w: pl.BlockSpec(
        (1, 1, A_RADIUS, A_WIDTH),
        lambda b, rr, m: (b, rr, jnp.minimum((m + 1) * hb, nhalo - 1), w))

    return pl.pallas_call(
        functools.partial(_attn_kernel, tq=tq, length=length, dil=dil),
        grid=(bsz, dil, nblk),
        in_specs=[main(0), prev(1), main(1), nxt(1), prev(2), main(2), nxt(2)],
        out_specs=[
            pl.BlockSpec((1, 1, tq, A_WIDTH), lambda b, rr, m: (b, rr, m, 0)),
            pl.BlockSpec((1, 1, tq, LANES), lambda b, rr, m: (b, rr, m, 0)),
        ],
        out_shape=[
            jax.ShapeDtypeStruct((bsz, dil, length, A_WIDTH), BF16),
            jax.ShapeDtypeStruct((bsz, dil, length, LANES), F32),
        ],
        scratch_shapes=[
            pltpu.VMEM((tq + 2 * A_RADIUS, A_WIDTH), BF16),
            pltpu.VMEM((tq + 2 * A_RADIUS, A_WIDTH), BF16),
            pltpu.VMEM((3 * A_HEADS, A_QB, A_SPAN), F32),
        ],
        compiler_params=_params("arbitrary", "arbitrary", "arbitrary"),
        name=f"dilated_attn_d{dil}",
    )(qkv, qkv, qkv, qkv, qkv, qkv, qkv)


def _amerge_kernel(o1_ref, o2_ref, o3_ref, l1_ref, l2_ref, l3_ref, gate_ref, x_ref,
                   e_ref, w_ref, gm_ref, lg_ref, lb_ref, out_ref, o_nat, l_nat, *, tm):
    for slot, (o_ref, l_ref) in enumerate(((o2_ref, l2_ref), (o3_ref, l3_ref))):
        dil = o_ref.shape[1]
        n = tm // dil
        for r in range(dil):
            o_r = o_ref[0, r].astype(F32)
            for cb in range(A_WIDTH // LANES):
                o_nat[slot, cb, pl.ds(r, n, stride=dil), :] = o_r[:, cb * LANES:(cb + 1) * LANES]
            l_nat[slot, pl.ds(r, n, stride=dil), :] = l_ref[0, r]
    natural = lambda slot: jnp.concatenate(
        [o_nat[slot, cb] for cb in range(A_WIDTH // LANES)], axis=1)
    l1, l2, l3 = l1_ref[0, 0], l_nat[0], l_nat[1]
    lmax = jnp.maximum(jnp.maximum(l1, l2), l3)
    e1, e2, e3 = jnp.exp(l1 - lmax), jnp.exp(l2 - lmax), jnp.exp(l3 - lmax)
    inv = 1.0 / (e1 + e2 + e3)
    expand = e_ref[...]
    acc = None
    for e, o in ((e1, o1_ref[0, 0].astype(F32)), (e2, natural(0)), (e3, natural(1))):
        wexp = _dot(jnp.concatenate(_split2(e * inv), axis=1), expand)
        term = wexp * o
        acc = term if acc is None else acc + term
    y = acc * _silu(gate_ref[0, 0].astype(F32))
    out = _dot(y.astype(BF16), w_ref[...])
    res = DEEPNORM_ALPHA * x_ref[0] + gm_ref[0] * out
    out_ref[0] = _layer_norm_rows(res, lg_ref[...], lb_ref[...])


def _amerge_out(os_, ls_, qkvg, x, w_out, gate_mod, ln_g, ln_b, tm):
    bsz, s, _ = x.shape
    expand = np.zeros((2, LANES, A_WIDTH), np.float32)
    for h in range(A_HEADS):
        expand[:, h, h * A_HEAD_DIM:(h + 1) * A_HEAD_DIM] = 1.0
    expand = jnp.asarray(expand.reshape(2 * LANES, A_WIDTH), BF16)
    tok = lambda w: pl.BlockSpec((1, tm, w), lambda b, i: (b, i, 0))

    def grp(arr, blk=0):
        dil, w = arr.shape[1], (A_WIDTH if arr.shape[3] > LANES else LANES)
        return pl.BlockSpec((1, dil, tm // dil, w), lambda b, i: (b, 0, i, blk))

    return pl.pallas_call(
        functools.partial(_amerge_kernel, tm=tm),
        grid=(bsz, s // tm),
        in_specs=[
            grp(os_[0]), grp(os_[1]), grp(os_[2]), grp(ls_[0]), grp(ls_[1]), grp(ls_[2]),
            grp(qkvg, 3),
            tok(D_MODEL),
            pl.BlockSpec((2 * LANES, A_WIDTH), lambda b, i: (0, 0)),
            pl.BlockSpec((A_WIDTH, D_MODEL), lambda b, i: (0, 0)),
            pl.BlockSpec((1, 1, D_MODEL), lambda b, i: (b, 0, 0)),
            pl.BlockSpec((1, D_MODEL), lambda b, i: (0, 0)),
            pl.BlockSpec((1, D_MODEL), lambda b, i: (0, 0)),
        ],
        out_specs=tok(D_MODEL),
        out_shape=jax.ShapeDtypeStruct((bsz, s, D_MODEL), F32),
        scratch_shapes=[
            pltpu.VMEM((2, A_WIDTH // LANES, tm, LANES), F32),
            pltpu.VMEM((2, tm, LANES), F32),
        ],
        compiler_params=_params("parallel", "parallel"),
        name="attn_merge_out",
    )(*os_, *ls_, qkvg, x, expand, w_out, gate_mod, ln_g, ln_b)


def _dt_kernel(x_ref, sh_ref, sc_ref, w_ref, bias_ref, alog_ref,
               dt_ref, cum_ref, dtt_ref, cumt_ref, *, tm):
    h = (x_ref[0] * (1.0 + sc_ref[0]) + sh_ref[0]).astype(BF16)
    raw = _dot(h, w_ref[...]) + bias_ref[...]
    dt = jnp.maximum(raw, 0.0) + jnp.log(1.0 + jnp.exp(-jnp.abs(raw)))
    a = dt * (-jnp.exp(alog_ref[...]))
    dt_ref[0] = dt
    ri = lax.broadcasted_iota(jnp.int32, (SSM_CHUNK, SSM_CHUNK), 0)
    ci = lax.broadcasted_iota(jnp.int32, (SSM_CHUNK, SSM_CHUNK), 1)
    tril = (ci <= ri).astype(BF16)
    triu = (ci >= ri).astype(BF16)
    lane = lax.broadcasted_iota(jnp.int32, (SSM_CHUNK, LANES), 1)
    fwd_cols = lane < SSM_HEADS
    for c in range(tm // SSM_CHUNK):
        rows = slice(c * SSM_CHUNK, (c + 1) * SSM_CHUNK)
        parts = _split3(a[rows])
        cum_f = sum(_dot(tril, p) for p in parts)
        cum_b = sum(_dot(triu, p) for p in parts)
        cum = jnp.where(fwd_cols, cum_f, cum_b)
        cum_ref[0, rows, :] = cum
        dtt_ref[0, c] = dt[rows].T
        cumt_ref[0, c] = cum.T


def _dt_path(x, shift, scale, w_dt, bias, a_log, tm):
    bsz, s, _ = x.shape
    nc = s // SSM_CHUNK
    cpb = tm // SSM_CHUNK
    tok = pl.BlockSpec((1, tm, LANES), lambda b, i: (b, i, 0))
    tr = pl.BlockSpec((1, cpb, LANES, SSM_CHUNK), lambda b, i: (b, i, 0, 0))
    row = pl.BlockSpec((1, LANES), lambda b, i: (0, 0))
    return pl.pallas_call(
        functools.partial(_dt_kernel, tm=tm),
        grid=(bsz, s // tm),
        in_specs=[
            pl.BlockSpec((1, tm, D_MODEL), lambda b, i: (b, i, 0)),
            pl.BlockSpec((1, 1, D_MODEL), lambda b, i: (b, 0, 0)),
            pl.BlockSpec((1, 1, D_MODEL), lambda b, i: (b, 0, 0)),
            pl.BlockSpec((D_MODEL, LANES), lambda b, i: (0, 0)),
            row, row,
        ],
        out_specs=[tok, tok, tr, tr],
        out_shape=[
            jax.ShapeDtypeStruct((bsz, s, LANES), F32),
            jax.ShapeDtypeStruct((bsz, s, LANES), F32),
            jax.ShapeDtypeStruct((bsz, nc, LANES, SSM_CHUNK), F32),
            jax.ShapeDtypeStruct((bsz, nc, LANES, SSM_CHUNK), F32),
        ],
        compiler_params=_params("parallel", "parallel"),
        name="ssd_dt",
    )(x, shift, scale, w_dt, bias, a_log)


def _conv_kernel(p_ref, m_ref, n_ref, s_ref, w_ref, b_ref, o_ref, cat, *, tm, nblk):
    i = pl.program_id(1)
    hal = BF16_ROWS
    cat[0:hal] = jnp.where(i > 0, p_ref[0], jnp.zeros_like(p_ref[0]))
    cat[hal:hal + tm] = m_ref[0]
    cat[hal + tm:hal + tm + hal] = jnp.where(i < nblk - 1, n_ref[0], jnp.zeros_like(n_ref[0]))
    half = SSM_CONV // 2
    taps = [k for k in range(SSM_CONV) if k != half]
    shifts = s_ref[...]
    for blk in range(tm // SSM_CHUNK):
        r0 = blk * SSM_CHUNK
        shifted = _dot(shifts, cat[r0:r0 + SSM_CHUNK + 2 * hal, :])
        acc = cat[hal + r0:hal + r0 + SSM_CHUNK, :].astype(F32) * w_ref[half:half + 1, :]
        for idx, k in enumerate(taps):
            acc = acc + shifted[idx * SSM_CHUNK:(idx + 1) * SSM_CHUNK] * w_ref[k:k + 1, :]
        o_ref[0, r0:r0 + SSM_CHUNK, :] = _silu(acc + b_ref[...]).astype(o_ref.dtype)


def _conv_silu(zx, conv_w, conv_b, tm):
    bsz, s, _ = zx.shape
    tc = 1024
    first = SSM_INNER // tc
    nblk = s // tm
    hb = tm // BF16_ROWS
    nh = s // BF16_ROWS
    half = SSM_CONV // 2
    win = SSM_CHUNK + 2 * BF16_ROWS
    shifts = np.zeros((SSM_CONV - 1, SSM_CHUNK, win), np.float32)
    for idx, k in enumerate(k for k in range(SSM_CONV) if k != half):
        shifts[idx, np.arange(SSM_CHUNK), np.arange(SSM_CHUNK) + BF16_ROWS + k - half] = 1.0
    shifts = jnp.asarray(shifts.reshape(-1, win), BF16)
    return pl.pallas_call(
        functools.partial(_conv_kernel, tm=tm, nblk=nblk),
        grid=(bsz, nblk, SSM_CONV_DIM // tc),
        in_specs=[
            pl.BlockSpec((1, BF16_ROWS, tc),
                         lambda b, i, j: (b, jnp.maximum(i * hb - 1, 0), first + j)),
            pl.BlockSpec((1, tm, tc), lambda b, i, j: (b, i, first + j)),
            pl.BlockSpec((1, BF16_ROWS, tc),
                         lambda b, i, j: (b, jnp.minimum((i + 1) * hb, nh - 1), first + j)),
            pl.BlockSpec(((SSM_CONV - 1) * SSM_CHUNK, win), lambda b, i, j: (0, 0)),
            pl.BlockSpec((SSM_CONV, tc), lambda b, i, j: (0, j)),
            pl.BlockSpec((1, tc), lambda b, i, j: (0, j)),
        ],
        out_specs=pl.BlockSpec((1, tm, tc), lambda b, i, j: (b, i, j)),
        out_shape=jax.ShapeDtypeStruct((bsz, s, SSM_CONV_DIM), BF16),
        scratch_shapes=[pltpu.VMEM((tm + 2 * BF16_ROWS, tc), BF16)],
        compiler_params=_params("parallel", "parallel", "parallel"),
        name="ssd_conv",
    )(zx, zx, zx, shifts, conv_w, conv_b.reshape(1, SSM_CONV_DIM))


def _ssd_kernel(*refs, reverse, cps):
    if reverse:
        (xs_ref, b_ref, c_ref, dt_ref, cum_ref, dtt_ref, cumt_ref, e_ref,
         yf_ref, z_ref, d_ref, nw_ref, y_ref, state) = refs
    else:
        (xs_ref, b_ref, c_ref, dt_ref, cum_ref, dtt_ref, cumt_ref, e_ref,
         y_ref, state) = refs
    off = SSM_HEADS if reverse else 0
    far = 0 if reverse else SSM_CHUNK - 1
    L = SSM_CHUNK

    @pl.when(pl.program_id(1) == 0)
    def _():
        state[...] = jnp.zeros_like(state)

    lane = lax.broadcasted_iota(jnp.int32, (L, LANES), 1)
    own = (lane >= off) & (lane < off + SSM_HEADS)
    lo_half = lane < SSM_HEAD_DIM
    ri = lax.broadcasted_iota(jnp.int32, (L, L), 0)
    ci = lax.broadcasted_iota(jnp.int32, (L, L), 1)
    mask = (ci >= ri) if reverse else (ci <= ri)

    def chunk(step, carry):
        cc = (cps - 1 - step) if reverse else step
        rows = pl.ds(pl.multiple_of(cc * L, L), L)
        xs = xs_ref[0, rows, :]
        dt = dt_ref[0, rows, :]
        cum = cum_ref[0, rows, :]
        dtt = dtt_ref[0, cc]
        cumt = cumt_ref[0, cc]
        tot = cum[far:far + 1, :]
        w_state = dt * jnp.exp(jnp.where(own, tot - cum, 0.0))
        e_cum = jnp.exp(jnp.where(own, cum, 0.0))
        ws_hi, ws_lo = _split2(w_state)
        ec_hi, ec_lo = _split2(e_cum)
        lhs = jnp.concatenate([jnp.concatenate([ws_hi, ws_lo], axis=1),
                               jnp.concatenate([ec_hi, ec_lo], axis=1)], axis=0)
        both = _dot(lhs, e_ref[...])
        ws_exp, ec_exp = both[:L], both[L:]
        xs_f = xs.astype(F32)
        xw = (xs_f * ws_exp).astype(BF16)
        col_e = cum * LOG2E
        row_e = (cumt - jnp.log(dtt)) * LOG2E

        y_groups = []
        for g in range(SSM_GROUPS):
            ns = slice(g * SSM_STATE, (g + 1) * SSM_STATE)
            gs = slice(g * SSM_GROUP_COLS, (g + 1) * SSM_GROUP_COLS)
            bg = b_ref[0, rows, ns]
            cg = c_ref[0, rows, ns]
            cb = _dot(cg, bg, _NT)
            st = state[g]
            y_off = _dot(cg, st.astype(BF16)) * ec_exp[:, gs]
            diag = []
            for pp in range(SSM_HEADS_PER_GROUP // 2):
                h0 = g * SSM_HEADS_PER_GROUP + 2 * pp
                lms = []
                for a in range(2):
                    col = off + h0 + a
                    diff = col_e[:, col:col + 1] - row_e[col:col + 1, :]
                    lmat = jnp.where(mask, jnp.exp2(diff) * cb, 0.0)
                    lms.append(lmat.astype(BF16))
                xp = xs[:, h0 * SSM_HEAD_DIM:(h0 + 2) * SSM_HEAD_DIM]
                zero = jnp.zeros_like(xp)
                rhs = jnp.concatenate([jnp.where(lo_half, xp, zero),
                                       jnp.where(lo_half, zero, xp)], axis=0)
                diag.append(_dot(jnp.concatenate(lms, axis=1), rhs))
            y_groups.append(jnp.concatenate(diag, axis=1) + y_off)
            state[g] = ec_exp[far:far + 1, gs] * st + _dot(bg, xw[:, gs], _TN)
        y = jnp.concatenate(y_groups, axis=1)

        if reverse:
            y = y + yf_ref[0, rows, :].astype(F32) + d_ref[...] * xs_f
            y = y * _silu(z_ref[0, rows, :].astype(F32))
            ms = jnp.mean(y * y, axis=-1, keepdims=True)
            y = y * lax.rsqrt(ms + RMS_EPS) * nw_ref[...]
        y_ref[0, rows, :] = y.astype(y_ref.dtype)
        return carry

    lax.fori_loop(0, cps, chunk, 0)


def _ssd_scan(xbc, dt, cum, dtt, cumt, reverse, cps, extra=None):
    bsz, s, _ = xbc.shape
    nblk = s // (cps * SSM_CHUNK)
    off = SSM_HEADS if reverse else 0
    expand = np.zeros((2, LANES, SSM_INNER), np.float32)
    for h in range(SSM_HEADS):
        expand[:, off + h, h * SSM_HEAD_DIM:(h + 1) * SSM_HEAD_DIM] = 1.0
    expand = jnp.asarray(expand.reshape(2 * LANES, SSM_INNER), BF16)
    cidx = (lambda c: nblk - 1 - c) if reverse else (lambda c: c)
    b_blk = SSM_INNER // (SSM_GROUPS * SSM_STATE)
    tok = lambda w, blk=0: pl.BlockSpec((1, cps * SSM_CHUNK, w), lambda b, c: (b, cidx(c), blk))
    tr = pl.BlockSpec((1, cps, LANES, SSM_CHUNK), lambda b, c: (b, cidx(c), 0, 0))
    in_specs = [
        tok(SSM_INNER), tok(SSM_GROUPS * SSM_STATE, b_blk), tok(SSM_GROUPS * SSM_STATE, b_blk + 1),
        tok(LANES), tok(LANES), tr, tr,
        pl.BlockSpec((2 * LANES, SSM_INNER), lambda b, c: (0, 0)),
    ]
    args = [xbc, xbc, xbc, dt, cum, dtt, cumt, expand]
    if reverse:
        yf, zx, d_exp, norm_w = extra
        row = pl.BlockSpec((1, SSM_INNER), lambda b, c: (0, 0))
        in_specs += [tok(SSM_INNER), tok(SSM_INNER), row, row]
        args += [yf, zx, d_exp, norm_w]
    return pl.pallas_call(
        functools.partial(_ssd_kernel, reverse=reverse, cps=cps),
        grid=(bsz, nblk),
        in_specs=in_specs,
        out_specs=tok(SSM_INNER),
        out_shape=jax.ShapeDtypeStruct((bsz, s, SSM_INNER), BF16),
        scratch_shapes=[pltpu.VMEM((SSM_GROUPS, SSM_STATE, SSM_GROUP_COLS), F32)],
        compiler_params=_params("parallel", "arbitrary"),
        name="ssd_bwd" if reverse else "ssd_fwd",
    )(*args)


def _out_kernel(y_ref, x_ref, w_ref, gm_ref, lg_ref, lb_ref, out_ref):
    out = _dot(y_ref[0], w_ref[...])
    res = DEEPNORM_ALPHA * x_ref[0] + gm_ref[0] * out
    out_ref[0] = _layer_norm_rows(res, lg_ref[...], lb_ref[...])


def _out_proj(y, x, w_out, gate_mod, ln_g, ln_b, tm):
    bsz, s, k = y.shape
    return pl.pallas_call(
        _out_kernel,
        grid=(bsz, s // tm),
        in_specs=[
            pl.BlockSpec((1, tm, k), lambda b, i: (b, i, 0)),
            pl.BlockSpec((1, tm, D_MODEL), lambda b, i: (b, i, 0)),
            pl.BlockSpec((k, D_MODEL), lambda b, i: (0, 0)),
            pl.BlockSpec((1, 1, D_MODEL), lambda b, i: (b, 0, 0)),
            pl.BlockSpec((1, D_MODEL), lambda b, i: (0, 0)),
            pl.BlockSpec((1, D_MODEL), lambda b, i: (0, 0)),
        ],
        out_specs=pl.BlockSpec((1, tm, D_MODEL), lambda b, i: (b, i, 0)),
        out_shape=jax.ShapeDtypeStruct((bsz, s, D_MODEL), F32),
        compiler_params=_params("parallel", "parallel"),
        name="out_proj_ln",
    )(y, x, w_out, gate_mod, ln_g, ln_b)


def _layer_a(x, shift, scale, gate_mod, w_in, w_out, ln_g, ln_b):
    gw = 3 * A_WIDTH
    w_bf = w_in.astype(BF16)
    os_, ls_ = [], []
    for g, (_, dil) in enumerate(DILATION_PAIRS):
        parts = [w_bf[:, g * gw:g * gw + A_WIDTH] * (1.0 / math.sqrt(A_HEAD_DIM)),
                 w_bf[:, g * gw + A_WIDTH:(g + 1) * gw]]
        if g == 0:
            parts.append(w_bf[:, N_DIL * gw:])
        w_g = jnp.concatenate(parts, axis=1)
        qkv = _inproj(x, shift, scale, w_g, tm=512, tn=1024, dil=dil)
        if g == 0:
            qkvg = qkv
        o, lse = _dilated_attention(qkv, tq=512)
        os_.append(o)
        ls_.append(lse)
    return _amerge_out(os_, ls_, qkvg, x, w_out.astype(BF16), gate_mod, ln_g, ln_b, tm=512)


def _layer_b(x, shift, scale, gate_mod, w_in, conv_w, conv_b, dt_bias, a_log, d_skip,
             norm_w, w_out, ln_g, ln_b):
    n_main = SSM_INNER + SSM_CONV_DIM
    zx = _inproj(x, shift, scale, w_in[:, :n_main].astype(BF16), tm=512, tn=1024)[:, 0]
    w_dt = jnp.zeros((D_MODEL, LANES), BF16).at[:, :2 * SSM_HEADS].set(
        w_in[:, n_main:].astype(BF16))
    pad = lambda v: jnp.zeros((1, LANES), F32).at[0, :2 * SSM_HEADS].set(v.reshape(-1))
    dt, cum, dtt, cumt = _dt_path(x, shift, scale, w_dt, pad(dt_bias), pad(a_log), tm=512)
    xbc = _conv_silu(zx, conv_w, conv_b, tm=512)
    yf = _ssd_scan(xbc, dt, cum, dtt, cumt, reverse=False, cps=4)
    d_exp = jnp.repeat(d_skip.astype(F32), SSM_HEAD_DIM).reshape(1, SSM_INNER)
    yn = _ssd_scan(xbc, dt, cum, dtt, cumt, reverse=True, cps=4,
                   extra=(yf, zx, d_exp, norm_w.reshape(1, SSM_INNER).astype(F32)))
    return _out_proj(yn, x, w_out.astype(BF16), gate_mod, ln_g, ln_b, tm=512)


def kernel(x, c, ada_w, ada_b, ln_g, ln_b, a_w_in, a_w_out, b_w_in, b_conv_w, b_conv_b,
           b_dt_bias, b_a_log, b_d, b_norm_w, b_w_out):
    bsz = x.shape[0]
    mod = _modulation(c, ada_w, ada_b)
    for i in range(DEPTH):
        m3 = mod[i].reshape(bsz, 3, 1, D_MODEL)
        shift, scale, gate_mod = m3[:, 0], m3[:, 1], m3[:, 2]
        lg = ln_g[i].reshape(1, D_MODEL)
        lb = ln_b[i].reshape(1, D_MODEL)
        j = i // 2
        if i % 2 == 0:
            x = _layer_a(x, shift, scale, gate_mod, a_w_in[j], a_w_out[j], lg, lb)
        else:
            x = _layer_b(x, shift, scale, gate_mod, b_w_in[j], b_conv_w[j], b_conv_b[j],
                         b_dt_bias[j], b_a_log[j], b_d[j], b_norm_w[j], b_w_out[j], lg, lb)
    return x
```

```python
import functools
import math

import jax
import jax.numpy as jnp
import numpy as np
from jax import lax
from jax.experimental import pallas as pl
from jax.experimental.pallas import tpu as pltpu

D_MODEL = 1024
DEPTH = 4

A_HEADS = 16
A_HEAD_DIM = 64
A_WIDTH = A_HEADS * A_HEAD_DIM
DILATION_PAIRS = ((128, 1), (512, 4), (2048, 16))
N_DIL = len(DILATION_PAIRS)
A_IN_COLS = N_DIL * 3 * A_WIDTH + A_WIDTH
A_RADIUS = 64
A_QB = 128
A_SPAN = A_QB + 2 * A_RADIUS

SSM_INNER = 2 * D_MODEL
SSM_HEAD_DIM = 64
SSM_HEADS = SSM_INNER // SSM_HEAD_DIM
SSM_STATE = 128
SSM_GROUPS = 4
SSM_CONV = 5
SSM_CHUNK = 128
SSM_CONV_DIM = SSM_INNER + 2 * SSM_GROUPS * SSM_STATE
SSM_GROUP_COLS = SSM_INNER // SSM_GROUPS
SSM_HEADS_PER_GROUP = SSM_HEADS // SSM_GROUPS

DEEPNORM_ALPHA = (2 * DEPTH) ** 0.25
LN_EPS = 1e-5
RMS_EPS = 1e-5

LANES = 128
BF16_ROWS = 16
MASKED_DIST = 1e30
LOG2E = math.log2(math.e)
CONV_STRIP = 512
VMEM_LIMIT = 48 * 1024 * 1024

F32 = jnp.float32
BF16 = jnp.bfloat16

_NT = (((1,), (1,)), ((), ()))
_TN = (((0,), (0,)), ((), ()))


def _params(*sem):
    return pltpu.CompilerParams(dimension_semantics=sem, vmem_limit_bytes=VMEM_LIMIT)


def _dot(a, b, dims=None):
    if dims is None:
        return jnp.dot(a, b, preferred_element_type=F32)
    return lax.dot_general(a, b, dims, preferred_element_type=F32)


def _split2(a):
    hi = a.astype(BF16)
    lo = (a - hi.astype(F32)).astype(BF16)
    return hi, lo


def _split3(a):
    hi = a.astype(BF16)
    r = a - hi.astype(F32)
    mid = r.astype(BF16)
    lo = (r - mid.astype(F32)).astype(BF16)
    return hi, mid, lo


def _silu(x):
    return x / (1.0 + jnp.exp2(x * (-LOG2E)))


def _layer_norm_rows(r, g, b):
    mu = jnp.mean(r, axis=-1, keepdims=True)
    d = r - mu
    var = jnp.mean(d * d, axis=-1, keepdims=True)
    return d * lax.rsqrt(var + LN_EPS) * g + b


def _mod_kernel(c_ref, w_ref, b_ref, o_ref):
    cond = _silu(c_ref[...])
    w = w_ref[0]
    c_hi, c_lo = _split2(cond)
    w_hi, w_lo = _split2(w)
    acc = _dot(c_hi, w_hi) + _dot(c_lo, w_hi) + _dot(c_hi, w_lo)
    o_ref[0] = acc + b_ref[0]


def _modulation(c, ada_w, ada_b):
    bsz = c.shape[0]
    rows = 8
    cp = jnp.zeros((rows, D_MODEL), F32).at[:bsz].set(c)
    tn = 1024
    out = pl.pallas_call(
        _mod_kernel,
        grid=(DEPTH, 3 * D_MODEL // tn),
        in_specs=[
            pl.BlockSpec((rows, D_MODEL), lambda i, j: (0, 0)),
            pl.BlockSpec((1, D_MODEL, tn), lambda i, j: (i, 0, j)),
            pl.BlockSpec((1, 1, tn), lambda i, j: (i, 0, j)),
        ],
        out_specs=pl.BlockSpec((1, rows, tn), lambda i, j: (i, 0, j)),
        out_shape=jax.ShapeDtypeStruct((DEPTH, rows, 3 * D_MODEL), F32),
        compiler_params=_params("parallel", "parallel"),
        name="ada_mod",
    )(cp, ada_w, ada_b.reshape(DEPTH, 1, 3 * D_MODEL))
    return out[:, :bsz]


def _dt_tail(hb, w_ref, bias_ref, alog_ref, dt_ref, cum_ref, dtt_ref, cumt_ref, tm):
    raw = _dot(hb, w_ref[...]) + bias_ref[...]
    dt = jnp.maximum(raw, 0.0) + jnp.log(1.0 + jnp.exp(-jnp.abs(raw)))
    a = dt * (-jnp.exp(alog_ref[...]))
    dt_ref[0] = dt
    ri = lax.broadcasted_iota(jnp.int32, (SSM_CHUNK, SSM_CHUNK), 0)
    ci = lax.broadcasted_iota(jnp.int32, (SSM_CHUNK, SSM_CHUNK), 1)
    tri = jnp.concatenate([(ci <= ri).astype(BF16), (ci >= ri).astype(BF16)], axis=0)
    lane = lax.broadcasted_iota(jnp.int32, (SSM_CHUNK, LANES), 1)
    fwd_cols = lane < SSM_HEADS
    for c in range(tm // SSM_CHUNK):
        rows = slice(c * SSM_CHUNK, (c + 1) * SSM_CHUNK)
        both = _dot(tri, jnp.concatenate(_split3(a[rows]), axis=1))
        both = both[:, :LANES] + both[:, LANES:2 * LANES] + both[:, 2 * LANES:]
        cum = jnp.where(fwd_cols, both[:SSM_CHUNK], both[SSM_CHUNK:])
        cum_ref[0, rows, :] = cum
        dtt_ref[0, c] = dt[rows].T
        cumt_ref[0, c] = cum.T


def _inproj_kernel(*refs, dil, tm, tn, with_dt):
    x_ref, sh_ref, sc_ref, w_ref = refs[:4]
    if with_dt:
        dt_in, (o_ref, *dt_out), (h_ref, *hf_ref) = refs[4:7], refs[7:12], refs[12:]
    else:
        o_ref, (h_ref, *hf_ref) = refs[4], refs[5:]
    n = tm // dil
    h = x_ref[0] * (1.0 + sc_ref[0]) + sh_ref[0]
    if dil == 1:
        h_ref[...] = h.astype(BF16)
    else:
        for cb in range(D_MODEL // LANES):
            cs = slice(cb * LANES, (cb + 1) * LANES)
            hf_ref[0][cb] = h[:, cs]
            for r in range(dil):
                h_ref[r * n:(r + 1) * n, cs] = (
                    hf_ref[0][cb, pl.ds(r, n, stride=dil), :].astype(BF16))

    hb = h_ref[...]
    for j in range(w_ref.shape[1] // tn):
        cols = slice(j * tn, (j + 1) * tn)
        res = _dot(hb, w_ref[:, cols])
        for r in range(dil):
            o_ref[0, r, :, cols] = res[r * n:(r + 1) * n].astype(o_ref.dtype)
    if with_dt:
        _dt_tail(hb, *dt_in, *dt_out, tm)


def _inproj(x, shift, scale, w, tm, tn, dil=1, dt_params=None):
    bsz, s, _ = x.shape
    n = w.shape[1]
    scratch = [pltpu.VMEM((tm, D_MODEL), BF16)]
    if dil > 1:
        scratch.append(pltpu.VMEM((D_MODEL // LANES, tm, LANES), F32))
    in_specs = [
        pl.BlockSpec((1, tm, D_MODEL), lambda b, i: (b, i, 0)),
        pl.BlockSpec((1, 1, D_MODEL), lambda b, i: (b, 0, 0)),
        pl.BlockSpec((1, 1, D_MODEL), lambda b, i: (b, 0, 0)),
        pl.BlockSpec((D_MODEL, n), lambda b, i: (0, 0), pipeline_mode=pl.Buffered(1)),
    ]
    out_specs = [pl.BlockSpec((1, dil, tm // dil, n), lambda b, i: (b, 0, i, 0))]
    out_shape = [jax.ShapeDtypeStruct((bsz, dil, s // dil, n), BF16)]
    args = [x, shift, scale, w]
    if dt_params is not None:
        assert dil == 1
        nc, cpb = s // SSM_CHUNK, tm // SSM_CHUNK
        row = pl.BlockSpec((1, LANES), lambda b, i: (0, 0))
        tok = pl.BlockSpec((1, tm, LANES), lambda b, i: (b, i, 0))
        tr = pl.BlockSpec((1, cpb, LANES, SSM_CHUNK), lambda b, i: (b, i, 0, 0))
        in_specs += [pl.BlockSpec((D_MODEL, LANES), lambda b, i: (0, 0)), row, row]
        out_specs += [tok, tok, tr, tr]
        out_shape += [jax.ShapeDtypeStruct((bsz, s, LANES), F32)] * 2
        out_shape += [jax.ShapeDtypeStruct((bsz, nc, LANES, SSM_CHUNK), F32)] * 2
        args += list(dt_params)
    outs = pl.pallas_call(
        functools.partial(_inproj_kernel, dil=dil, tm=tm, tn=tn, with_dt=dt_params is not None),
        grid=(bsz, s // tm),
        in_specs=in_specs,
        out_specs=out_specs,
        out_shape=out_shape,
        scratch_shapes=scratch,
        compiler_params=_params("parallel", "parallel"),
        name=f"inproj_d{dil}",
    )(*args)
    return outs if dt_params is not None else outs[0]


def _attn_kernel(q_ref, kp_ref, k_ref, kn_ref, vp_ref, v_ref, vn_ref,
                 o_ref, lse_ref, kcat, vcat, bias, *, tq, length, dil):
    m = pl.program_id(2)
    r = A_RADIUS
    kcat[0:r] = kp_ref[0, 0]
    kcat[r:r + tq] = k_ref[0, 0]
    kcat[r + tq:r + tq + r] = kn_ref[0, 0]
    vcat[0:r] = vp_ref[0, 0]
    vcat[r:r + tq] = v_ref[0, 0]
    vcat[r + tq:r + tq + r] = vn_ref[0, 0]

    @pl.when((pl.program_id(0) == 0) & (pl.program_id(1) == 0) & (m == 0))
    def _():
        row = lax.broadcasted_iota(jnp.int32, (A_QB, A_SPAN), 0)
        col = lax.broadcasted_iota(jnp.int32, (A_QB, A_SPAN), 1)
        adist = jnp.abs(col - r - row)
        band = adist <= r
        for t, valid in enumerate((band, band & (col >= r), band & (col < A_SPAN - r))):
            dist = jnp.where(valid, adist.astype(F32) * float(dil), MASKED_DIST)
            for h in range(A_HEADS):
                bias[t * A_HEADS + h] = -(LOG2E * 2.0 ** (-8.0 * (h + 1) / A_HEADS)) * dist

    lane = lax.broadcasted_iota(jnp.int32, (A_QB, LANES), 1)
    lo_half = lane < A_HEAD_DIM
    ones = jnp.ones((A_SPAN, LANES), BF16)

    def sub_block(j, carry):
        r0 = pl.multiple_of(j * A_QB, A_QB)
        base = m * tq + j * A_QB - r
        variant = jnp.where(base < 0, 1, jnp.where(base + A_SPAN > length, 2, 0))
        m_tile = jnp.zeros((A_QB, LANES), F32)
        z_tile = jnp.ones((A_QB, LANES), F32)
        for hp in range(A_HEADS // 2):
            cs = slice(hp * LANES, (hp + 1) * LANES)
            q_pair = q_ref[0, 0, pl.ds(r0, A_QB), cs]
            k_pair = kcat[pl.ds(r0, A_SPAN), cs]
            v_ext = jnp.concatenate([vcat[pl.ds(r0, A_SPAN), cs], ones], axis=1)
            outs = []
            for a in range(2):
                h = 2 * hp + a
                keep = lo_half if a == 0 else jnp.logical_not(lo_half)
                qm = jnp.where(keep, q_pair, jnp.zeros_like(q_pair))
                sc = _dot(qm, k_pair, _NT) + bias[variant * A_HEADS + h]
                mx = jnp.max(sc, axis=-1, keepdims=True)
                p = jnp.exp2(sc - mx)
                oe = _dot(p.astype(BF16), v_ext)
                z = oe[:, LANES:]
                outs.append(oe[:, :LANES] / z)
                m_tile = jnp.where(lane == h, mx, m_tile)
                z_tile = jnp.where(lane == h, z, z_tile)
            o_pair = jnp.where(lo_half, outs[0], outs[1])
            o_ref[0, 0, pl.ds(r0, A_QB), cs] = o_pair.astype(o_ref.dtype)
        lse_ref[0, 0, pl.ds(r0, A_QB), :] = (m_tile + jnp.log2(z_tile)) * (1.0 / LOG2E)
        return carry

    lax.fori_loop(0, tq // A_QB, sub_block, 0)


def _dilated_attention(qkv, tq):
    bsz, dil, length, _ = qkv.shape
    assert length >= 2 * A_QB, "a score tile may touch only one end of the sequence"
    tq = min(tq, length)
    nblk = length // tq
    hb = tq // A_RADIUS
    nhalo = length // A_RADIUS

    main = lambda w: pl.BlockSpec((1, 1, tq, A_WIDTH), lambda b, rr, m: (b, rr, m, w))
    prev = lambda w: pl.BlockSpec(
        (1, 1, A_RADIUS, A_WIDTH), lambda b, rr, m: (b, rr, jnp.maximum(m * hb - 1, 0), w))
    nxt = lambda w: pl.BlockSpec(
        (1, 1, A_RADIUS, A_WIDTH),
        lambda b, rr, m: (b, rr, jnp.minimum((m + 1) * hb, nhalo - 1), w))

    return pl.pallas_call(
        functools.partial(_attn_kernel, tq=tq, length=length, dil=dil),
        grid=(bsz, dil, nblk),
        in_specs=[main(0), prev(1), main(1), nxt(1), prev(2), main(2), nxt(2)],
        out_specs=[
            pl.BlockSpec((1, 1, tq, A_WIDTH), lambda b, rr, m: (b, rr, m, 0)),
            pl.BlockSpec((1, 1, tq, LANES), lambda b, rr, m: (b, rr, m, 0)),
        ],
        out_shape=[
            jax.ShapeDtypeStruct((bsz, dil, length, A_WIDTH), BF16),
            jax.ShapeDtypeStruct((bsz, dil, length, LANES), F32),
        ],
        scratch_shapes=[
            pltpu.VMEM((tq + 2 * A_RADIUS, A_WIDTH), BF16),
            pltpu.VMEM((tq + 2 * A_RADIUS, A_WIDTH), BF16),
            pltpu.VMEM((3 * A_HEADS, A_QB, A_SPAN), F32),
        ],
        compiler_params=_params("arbitrary", "arbitrary", "arbitrary"),
        name=f"dilated_attn_d{dil}",
    )(qkv, qkv, qkv, qkv, qkv, qkv, qkv)


def _amerge_kernel(o1_ref, o2_ref, o3_ref, l1_ref, l2_ref, l3_ref, gate_ref, x_ref,
                   e_ref, w_ref, gm_ref, lg_ref, lb_ref, out_ref, o_nat, l_nat, *, tm):
    for slot, (o_ref, l_ref) in enumerate(((o2_ref, l2_ref), (o3_ref, l3_ref))):
        dil = o_ref.shape[1]
        n = tm // dil
        for r in range(dil):
            o_r = o_ref[0, r].astype(F32)
            for cb in range(A_WIDTH // LANES):
                o_nat[slot, cb, pl.ds(r, n, stride=dil), :] = o_r[:, cb * LANES:(cb + 1) * LANES]
            l_nat[slot, pl.ds(r, n, stride=dil), :] = l_ref[0, r]
    natural = lambda slot: jnp.concatenate(
        [o_nat[slot, cb] for cb in range(A_WIDTH // LANES)], axis=1)
    l1, l2, l3 = l1_ref[0, 0], l_nat[0], l_nat[1]
    lmax = jnp.maximum(jnp.maximum(l1, l2), l3)
    e1, e2, e3 = jnp.exp(l1 - lmax), jnp.exp(l2 - lmax), jnp.exp(l3 - lmax)
    inv = 1.0 / (e1 + e2 + e3)
    expand = e_ref[...]
    acc = None
    for e, o in ((e1, o1_ref[0, 0].astype(F32)), (e2, natural(0)), (e3, natural(1))):
        wexp = _dot(jnp.concatenate(_split2(e * inv), axis=1), expand)
        term = wexp * o
        acc = term if acc is None else acc + term
    y = acc * _silu(gate_ref[0, 0].astype(F32))
    out = _dot(y.astype(BF16), w_ref[...])
    res = DEEPNORM_ALPHA * x_ref[0] + gm_ref[0] * out
    out_ref[0] = _layer_norm_rows(res, lg_ref[...], lb_ref[...])


def _amerge_out(os_, ls_, qkvg, x, w_out, gate_mod, ln_g, ln_b, tm):
    bsz, s, _ = x.shape
    expand = np.zeros((2, LANES, A_WIDTH), np.float32)
    for h in range(A_HEADS):
        expand[:, h, h * A_HEAD_DIM:(h + 1) * A_HEAD_DIM] = 1.0
    expand = jnp.asarray(expand.reshape(2 * LANES, A_WIDTH), BF16)
    tok = lambda w: pl.BlockSpec((1, tm, w), lambda b, i: (b, i, 0))

    def grp(arr, blk=0):
        dil, w = arr.shape[1], (A_WIDTH if arr.shape[3] > LANES else LANES)
        return pl.BlockSpec((1, dil, tm // dil, w), lambda b, i: (b, 0, i, blk))

    return pl.pallas_call(
        functools.partial(_amerge_kernel, tm=tm),
        grid=(bsz, s // tm),
        in_specs=[
            grp(os_[0]), grp(os_[1]), grp(os_[2]), grp(ls_[0]), grp(ls_[1]), grp(ls_[2]),
            grp(qkvg, 3),
            tok(D_MODEL),
            pl.BlockSpec((2 * LANES, A_WIDTH), lambda b, i: (0, 0)),
            pl.BlockSpec((A_WIDTH, D_MODEL), lambda b, i: (0, 0)),
            pl.BlockSpec((1, 1, D_MODEL), lambda b, i: (b, 0, 0)),
            pl.BlockSpec((1, D_MODEL), lambda b, i: (0, 0)),
            pl.BlockSpec((1, D_MODEL), lambda b, i: (0, 0)),
        ],
        out_specs=tok(D_MODEL),
        out_shape=jax.ShapeDtypeStruct((bsz, s, D_MODEL), F32),
        scratch_shapes=[
            pltpu.VMEM((2, A_WIDTH // LANES, tm, LANES), F32),
            pltpu.VMEM((2, tm, LANES), F32),
        ],
        compiler_params=_params("parallel", "parallel"),
        name="attn_merge_out",
    )(*os_, *ls_, qkvg, x, expand, w_out, gate_mod, ln_g, ln_b)


def _conv_kernel(p_ref, m_ref, n_ref, s_ref, w_ref, b_ref, o_ref, cat, *, tm, nblk):
    i = pl.program_id(1)
    hal = BF16_ROWS
    cat[0:hal] = jnp.where(i > 0, p_ref[0], jnp.zeros_like(p_ref[0]))
    cat[hal:hal + tm] = m_ref[0]
    cat[hal + tm:hal + tm + hal] = jnp.where(i < nblk - 1, n_ref[0], jnp.zeros_like(n_ref[0]))
    half = SSM_CONV // 2
    taps = [k for k in range(SSM_CONV) if k != half]

    def row_block(blk, carry):
        r0 = pl.multiple_of(blk * SSM_CHUNK, SSM_CHUNK)
        for strip in range(o_ref.shape[2] // CONV_STRIP):
            cs = slice(strip * CONV_STRIP, (strip + 1) * CONV_STRIP)
            win = cat[pl.ds(r0, SSM_CHUNK + 2 * hal), cs]
            centre = cat[pl.ds(pl.multiple_of(r0 + hal, hal), SSM_CHUNK), cs]
            acc = centre.astype(F32) * w_ref[half:half + 1, cs]
            shifted = _dot(s_ref[...], win)
            for idx, k in enumerate(taps):
                acc = acc + shifted[idx * SSM_CHUNK:(idx + 1) * SSM_CHUNK] * w_ref[k:k + 1, cs]
            o_ref[0, pl.ds(r0, SSM_CHUNK), cs] = _silu(acc + b_ref[:, cs]).astype(o_ref.dtype)
        return carry

    lax.fori_loop(0, tm // SSM_CHUNK, row_block, 0, unroll=True)


def _conv_silu(zx, conv_w, conv_b, tm):
    bsz, s, _ = zx.shape
    tc = 1024
    first = SSM_INNER // tc
    nblk = s // tm
    hb = tm // BF16_ROWS
    nh = s // BF16_ROWS
    half = SSM_CONV // 2
    win = SSM_CHUNK + 2 * BF16_ROWS
    shifts = np.zeros((SSM_CONV - 1, SSM_CHUNK, win), np.float32)
    for idx, k in enumerate(k for k in range(SSM_CONV) if k != half):
        shifts[idx, np.arange(SSM_CHUNK), np.arange(SSM_CHUNK) + BF16_ROWS + k - half] = 1.0
    shifts = jnp.asarray(shifts.reshape(-1, win), BF16)
    return pl.pallas_call(
        functools.partial(_conv_kernel, tm=tm, nblk=nblk),
        grid=(bsz, nblk, SSM_CONV_DIM // tc),
        in_specs=[
            pl.BlockSpec((1, BF16_ROWS, tc),
                         lambda b, i, j: (b, jnp.maximum(i * hb - 1, 0), first + j)),
            pl.BlockSpec((1, tm, tc), lambda b, i, j: (b, i, first + j)),
            pl.BlockSpec((1, BF16_ROWS, tc),
                         lambda b, i, j: (b, jnp.minimum((i + 1) * hb, nh - 1), first + j)),
            pl.BlockSpec(((SSM_CONV - 1) * SSM_CHUNK, win), lambda b, i, j: (0, 0)),
            pl.BlockSpec((SSM_CONV, tc), lambda b, i, j: (0, j)),
            pl.BlockSpec((1, tc), lambda b, i, j: (0, j)),
        ],
        out_specs=pl.BlockSpec((1, tm, tc), lambda b, i, j: (b, i, j)),
        out_shape=jax.ShapeDtypeStruct((bsz, s, SSM_CONV_DIM), BF16),
        scratch_shapes=[pltpu.VMEM((tm + 2 * BF16_ROWS, tc), BF16)],
        compiler_params=_params("parallel", "parallel", "parallel"),
        name="ssd_conv",
    )(zx, zx, zx, shifts, conv_w, conv_b.reshape(1, SSM_CONV_DIM))


def _ssd_kernel(*refs, reverse, cps):
    if reverse:
        (xs_ref, b_ref, c_ref, dt_ref, cum_ref, dtt_ref, cumt_ref, e_ref,
         yf_ref, z_ref, d_ref, nw_ref, y_ref, state) = refs
    else:
        (xs_ref, b_ref, c_ref, dt_ref, cum_ref, dtt_ref, cumt_ref, e_ref,
         y_ref, state) = refs
    off = SSM_HEADS if reverse else 0
    far = 0 if reverse else SSM_CHUNK - 1
    L = SSM_CHUNK

    @pl.when(pl.program_id(1) == 0)
    def _():
        state[...] = jnp.zeros_like(state)

    lane = lax.broadcasted_iota(jnp.int32, (L, LANES), 1)
    own = (lane >= off) & (lane < off + SSM_HEADS)
    lo_half = lane < SSM_HEAD_DIM
    ri = lax.broadcasted_iota(jnp.int32, (L, L), 0)
    ci = lax.broadcasted_iota(jnp.int32, (L, L), 1)
    mask = (ci >= ri) if reverse else (ci <= ri)

    def chunk(step, carry):
        cc = (cps - 1 - step) if reverse else step
        rows = pl.ds(pl.multiple_of(cc * L, L), L)
        xs = xs_ref[0, rows, :]
        dt = dt_ref[0, rows, :]
        cum = cum_ref[0, rows, :]
        dtt = dtt_ref[0, cc]
        cumt = cumt_ref[0, cc]
        tot = cum[far:far + 1, :]
        w_state = dt * jnp.exp(jnp.where(own, tot - cum, 0.0))
        e_cum = jnp.exp(jnp.where(own, cum, 0.0))
        ws_hi, ws_lo = _split2(w_state)
        ec_hi, ec_lo = _split2(e_cum)
        lhs = jnp.concatenate([jnp.concatenate([ws_hi, ws_lo], axis=1),
                               jnp.concatenate([ec_hi, ec_lo], axis=1)], axis=0)
        both = _dot(lhs, e_ref[...])
        ws_exp, ec_exp = both[:L], both[L:]
        xs_f = xs.astype(F32)
        xw = (xs_f * ws_exp).astype(BF16)
        col_e = cum * LOG2E
        row_e = (cumt - jnp.log(dtt)) * LOG2E

        y_groups = []
        for g in range(SSM_GROUPS):
            ns = slice(g * SSM_STATE, (g + 1) * SSM_STATE)
            gs = slice(g * SSM_GROUP_COLS, (g + 1) * SSM_GROUP_COLS)
            bg = b_ref[0, rows, ns]
            cg = c_ref[0, rows, ns]
            cb = _dot(cg, bg, _NT)
            st = state[g]
            y_off = _dot(cg, st.astype(BF16)) * ec_exp[:, gs]
            diag = []
            for pp in range(SSM_HEADS_PER_GROUP // 2):
                h0 = g * SSM_HEADS_PER_GROUP + 2 * pp
                lms = []
                for a in range(2):
                    col = off + h0 + a
                    diff = col_e[:, col:col + 1] - row_e[col:col + 1, :]
                    lmat = jnp.where(mask, jnp.exp2(diff) * cb, 0.0)
                    lms.append(lmat.astype(BF16))
                xp = xs[:, h0 * SSM_HEAD_DIM:(h0 + 2) * SSM_HEAD_DIM]
                zero = jnp.zeros_like(xp)
                rhs = jnp.concatenate([jnp.where(lo_half, xp, zero),
                                       jnp.where(lo_half, zero, xp)], axis=0)
                diag.append(_dot(jnp.concatenate(lms, axis=1), rhs))
            y_groups.append(jnp.concatenate(diag, axis=1) + y_off)
            state[g] = ec_exp[far:far + 1, gs] * st + _dot(bg, xw[:, gs], _TN)
        y = jnp.concatenate(y_groups, axis=1)

        if reverse:
            y = y + yf_ref[0, rows, :].astype(F32) + d_ref[...] * xs_f
            y = y * _silu(z_ref[0, rows, :].astype(F32))
            ms = jnp.mean(y * y, axis=-1, keepdims=True)
            y = y * lax.rsqrt(ms + RMS_EPS) * nw_ref[...]
        y_ref[0, rows, :] = y.astype(y_ref.dtype)
        return carry

    lax.fori_loop(0, cps, chunk, 0, unroll=2)


def _ssd_scan(xbc, dt, cum, dtt, cumt, reverse, cps, extra=None):
    bsz, s, _ = xbc.shape
    nblk = s // (cps * SSM_CHUNK)
    off = SSM_HEADS if reverse else 0
    expand = np.zeros((2, LANES, SSM_INNER), np.float32)
    for h in range(SSM_HEADS):
        expand[:, off + h, h * SSM_HEAD_DIM:(h + 1) * SSM_HEAD_DIM] = 1.0
    expand = jnp.asarray(expand.reshape(2 * LANES, SSM_INNER), BF16)
    cidx = (lambda c: nblk - 1 - c) if reverse else (lambda c: c)
    b_blk = SSM_INNER // (SSM_GROUPS * SSM_STATE)
    tok = lambda w, blk=0: pl.BlockSpec((1, cps * SSM_CHUNK, w), lambda b, c: (b, cidx(c), blk))
    tr = pl.BlockSpec((1, cps, LANES, SSM_CHUNK), lambda b, c: (b, cidx(c), 0, 0))
    in_specs = [
        tok(SSM_INNER), tok(SSM_GROUPS * SSM_STATE, b_blk), tok(SSM_GROUPS * SSM_STATE, b_blk + 1),
        tok(LANES), tok(LANES), tr, tr,
        pl.BlockSpec((2 * LANES, SSM_INNER), lambda b, c: (0, 0)),
    ]
    args = [xbc, xbc, xbc, dt, cum, dtt, cumt, expand]
    if reverse:
        yf, zx, d_exp, norm_w = extra
        row = pl.BlockSpec((1, SSM_INNER), lambda b, c: (0, 0))
        in_specs += [tok(SSM_INNER), tok(SSM_INNER), row, row]
        args += [yf, zx, d_exp, norm_w]
    return pl.pallas_call(
        functools.partial(_ssd_kernel, reverse=reverse, cps=cps),
        grid=(bsz, nblk),
        in_specs=in_specs,
        out_specs=tok(SSM_INNER),
        out_shape=jax.ShapeDtypeStruct((bsz, s, SSM_INNER), BF16),
        scratch_shapes=[pltpu.VMEM((SSM_GROUPS, SSM_STATE, SSM_GROUP_COLS), F32)],
        compiler_params=_params("parallel", "arbitrary"),
        name="ssd_bwd" if reverse else "ssd_fwd",
    )(*args)


def _out_kernel(y_ref, x_ref, w_ref, gm_ref, lg_ref, lb_ref, out_ref):
    out = _dot(y_ref[0], w_ref[...])
    res = DEEPNORM_ALPHA * x_ref[0] + gm_ref[0] * out
    out_ref[0] = _layer_norm_rows(res, lg_ref[...], lb_ref[...])


def _out_proj(y, x, w_out, gate_mod, ln_g, ln_b, tm):
    bsz, s, k = y.shape
    return pl.pallas_call(
        _out_kernel,
        grid=(bsz, s // tm),
        in_specs=[
            pl.BlockSpec((1, tm, k), lambda b, i: (b, i, 0)),
            pl.BlockSpec((1, tm, D_MODEL), lambda b, i: (b, i, 0)),
            pl.BlockSpec((k, D_MODEL), lambda b, i: (0, 0)),
            pl.BlockSpec((1, 1, D_MODEL), lambda b, i: (b, 0, 0)),
            pl.BlockSpec((1, D_MODEL), lambda b, i: (0, 0)),
            pl.BlockSpec((1, D_MODEL), lambda b, i: (0, 0)),
        ],
        out_specs=pl.BlockSpec((1, tm, D_MODEL), lambda b, i: (b, i, 0)),
        out_shape=jax.ShapeDtypeStruct((bsz, s, D_MODEL), F32),
        compiler_params=_params("parallel", "parallel"),
        name="out_proj_ln",
    )(y, x, w_out, gate_mod, ln_g, ln_b)


def _layer_a(x, shift, scale, gate_mod, w_in, w_out, ln_g, ln_b):
    gw = 3 * A_WIDTH
    w_bf = w_in.astype(BF16)
    os_, ls_ = [], []
    for g, (_, dil) in enumerate(DILATION_PAIRS):
        q_scale = LOG2E / math.sqrt(A_HEAD_DIM)
        parts = [(w_in[:, g * gw:g * gw + A_WIDTH] * q_scale).astype(BF16),
                 w_bf[:, g * gw + A_WIDTH:(g + 1) * gw]]
        if g == 0:
            parts.append(w_bf[:, N_DIL * gw:])
        w_g = jnp.concatenate(parts, axis=1)
        qkv = _inproj(x, shift, scale, w_g, tm=512, tn=1024, dil=dil)
        if g == 0:
            qkvg = qkv
        o, lse = _dilated_attention(qkv, tq=512)
        os_.append(o)
        ls_.append(lse)
    return _amerge_out(os_, ls_, qkvg, x, w_out.astype(BF16), gate_mod, ln_g, ln_b, tm=512)


def _layer_b(x, shift, scale, gate_mod, w_in, conv_w, conv_b, dt_bias, a_log, d_skip,
             norm_w, w_out, ln_g, ln_b):
    n_main = SSM_INNER + SSM_CONV_DIM
    w_dt = jnp.zeros((D_MODEL, LANES), BF16).at[:, :2 * SSM_HEADS].set(
        w_in[:, n_main:].astype(BF16))
    pad = lambda v: jnp.zeros((1, LANES), F32).at[0, :2 * SSM_HEADS].set(v.reshape(-1))
    zx, dt, cum, dtt, cumt = _inproj(x, shift, scale, w_in[:, :n_main].astype(BF16), tm=512,
                                     tn=1024, dt_params=(w_dt, pad(dt_bias), pad(a_log)))
    zx = zx[:, 0]
    xbc = _conv_silu(zx, conv_w, conv_b, tm=512)
    yf = _ssd_scan(xbc, dt, cum, dtt, cumt, reverse=False, cps=4)
    d_exp = jnp.repeat(d_skip.astype(F32), SSM_HEAD_DIM).reshape(1, SSM_INNER)
    yn = _ssd_scan(xbc, dt, cum, dtt, cumt, reverse=True, cps=4,
                   extra=(yf, zx, d_exp, norm_w.reshape(1, SSM_INNER).astype(F32)))
    return _out_proj(yn, x, w_out.astype(BF16), gate_mod, ln_g, ln_b, tm=512)


def kernel(x, c, ada_w, ada_b, ln_g, ln_b, a_w_in, a_w_out, b_w_in, b_conv_w, b_conv_b,
           b_dt_bias, b_a_log, b_d, b_norm_w, b_w_out):
    bsz = x.shape[0]
    mod = _modulation(c, ada_w, ada_b)
    for i in range(DEPTH):
        m3 = mod[i].reshape(bsz, 3, 1, D_MODEL)
        shift, scale, gate_mod = m3[:, 0], m3[:, 1], m3[:, 2]
        lg = ln_g[i].reshape(1, D_MODEL)
        lb = ln_b[i].reshape(1, D_MODEL)
        j = i // 2
        if i % 2 == 0:
            x = _layer_a(x, shift, scale, gate_mod, a_w_in[j], a_w_out[j], lg, lb)
        else:
            x = _layer_b(x, shift, scale, gate_mod, b_w_in[j], b_conv_w[j], b_conv_b[j],
                         b_dt_bias[j], b_a_log[j], b_d[j], b_norm_w[j], b_w_out[j], lg, lb)
    return x
```

```python
import functools
import math

import jax
import jax.numpy as jnp
import numpy as np
from jax import lax
from jax.experimental import pallas as pl
from jax.experimental.pallas import tpu as pltpu

D_MODEL = 1024
DEPTH = 4

A_HEADS = 16
A_HEAD_DIM = 64
A_WIDTH = A_HEADS * A_HEAD_DIM
DILATION_PAIRS = ((128, 1), (512, 4), (2048, 16))
N_DIL = len(DILATION_PAIRS)
A_IN_COLS = N_DIL * 3 * A_WIDTH + A_WIDTH
A_RADIUS = 64
A_QB = 128
A_SPAN = A_QB + 2 * A_RADIUS

SSM_INNER = 2 * D_MODEL
SSM_HEAD_DIM = 64
SSM_HEADS = SSM_INNER // SSM_HEAD_DIM
SSM_STATE = 128
SSM_GROUPS = 4
SSM_CONV = 5
SSM_CHUNK = 128
SSM_CONV_DIM = SSM_INNER + 2 * SSM_GROUPS * SSM_STATE
SSM_GROUP_COLS = SSM_INNER // SSM_GROUPS
SSM_HEADS_PER_GROUP = SSM_HEADS // SSM_GROUPS

DEEPNORM_ALPHA = (2 * DEPTH) ** 0.25
LN_EPS = 1e-5
RMS_EPS = 1e-5

LANES = 128
BF16_ROWS = 16
MASKED_DIST = 1e30
LOG2E = math.log2(math.e)
CONV_STRIP = 512
VMEM_LIMIT = 48 * 1024 * 1024

F32 = jnp.float32
BF16 = jnp.bfloat16

_NT = (((1,), (1,)), ((), ()))
_TN = (((0,), (0,)), ((), ()))


def _params(*sem):
    return pltpu.CompilerParams(dimension_semantics=sem, vmem_limit_bytes=VMEM_LIMIT)


def _dot(a, b, dims=None):
    if dims is None:
        return jnp.dot(a, b, preferred_element_type=F32)
    return lax.dot_general(a, b, dims, preferred_element_type=F32)


def _split2(a):
    hi = a.astype(BF16)
    lo = (a - hi.astype(F32)).astype(BF16)
    return hi, lo


def _split3(a):
    hi = a.astype(BF16)
    r = a - hi.astype(F32)
    mid = r.astype(BF16)
    lo = (r - mid.astype(F32)).astype(BF16)
    return hi, mid, lo


def _silu(x):
    return x / (1.0 + jnp.exp2(x * (-LOG2E)))


def _layer_norm_rows(r, g, b):
    mu = jnp.mean(r, axis=-1, keepdims=True)
    d = r - mu
    var = jnp.mean(d * d, axis=-1, keepdims=True)
    return d * lax.rsqrt(var + LN_EPS) * g + b


def _mod_kernel(c_ref, w_ref, b_ref, o_ref):
    cond = _silu(c_ref[...])
    w = w_ref[0]
    c_hi, c_lo = _split2(cond)
    w_hi, w_lo = _split2(w)
    acc = _dot(c_hi, w_hi) + _dot(c_lo, w_hi) + _dot(c_hi, w_lo)
    o_ref[0] = acc + b_ref[0]


def _modulation(c, ada_w, ada_b):
    bsz = c.shape[0]
    rows = 8
    cp = jnp.zeros((rows, D_MODEL), F32).at[:bsz].set(c)
    tn = 1024
    out = pl.pallas_call(
        _mod_kernel,
        grid=(DEPTH, 3 * D_MODEL // tn),
        in_specs=[
            pl.BlockSpec((rows, D_MODEL), lambda i, j: (0, 0)),
            pl.BlockSpec((1, D_MODEL, tn), lambda i, j: (i, 0, j)),
            pl.BlockSpec((1, 1, tn), lambda i, j: (i, 0, j)),
        ],
        out_specs=pl.BlockSpec((1, rows, tn), lambda i, j: (i, 0, j)),
        out_shape=jax.ShapeDtypeStruct((DEPTH, rows, 3 * D_MODEL), F32),
        compiler_params=_params("parallel", "parallel"),
        name="ada_mod",
    )(cp, ada_w, ada_b.reshape(DEPTH, 1, 3 * D_MODEL))
    return out[:, :bsz]


def _dt_tail(hb, w_ref, bias_ref, alog_ref, dt_ref, cum_ref, dtt_ref, cumt_ref, tm):
    raw = _dot(hb, w_ref[...]) + bias_ref[...]
    dt = jnp.maximum(raw, 0.0) + jnp.log(1.0 + jnp.exp(-jnp.abs(raw)))
    a = dt * (-jnp.exp(alog_ref[...]))
    dt_ref[0] = dt
    ri = lax.broadcasted_iota(jnp.int32, (SSM_CHUNK, SSM_CHUNK), 0)
    ci = lax.broadcasted_iota(jnp.int32, (SSM_CHUNK, SSM_CHUNK), 1)
    tri = jnp.concatenate([(ci <= ri).astype(BF16), (ci >= ri).astype(BF16)], axis=0)
    lane = lax.broadcasted_iota(jnp.int32, (SSM_CHUNK, LANES), 1)
    fwd_cols = lane < SSM_HEADS
    for c in range(tm // SSM_CHUNK):
        rows = slice(c * SSM_CHUNK, (c + 1) * SSM_CHUNK)
        both = _dot(tri, jnp.concatenate(_split3(a[rows]), axis=1))
        both = both[:, :LANES] + both[:, LANES:2 * LANES] + both[:, 2 * LANES:]
        cum = jnp.where(fwd_cols, both[:SSM_CHUNK], both[SSM_CHUNK:])
        cum_ref[0, rows, :] = cum
        dtt_ref[0, c] = dt[rows].T
        cumt_ref[0, c] = cum.T


def _inproj_kernel(*refs, dil, tm, tn, n_w, with_dt, lead_scale):
    x_ref, sh_ref, sc_ref = refs[:3]
    w_refs, refs = refs[3:3 + n_w], refs[3 + n_w:]
    if with_dt:
        dt_in, refs = refs[:3], refs[3:]
    o_refs, refs = refs[:n_w], refs[n_w:]
    if with_dt:
        dt_out, refs = refs[:4], refs[4:]
    h_ref, *hf_ref = refs
    n = tm // dil
    h = x_ref[0] * (1.0 + sc_ref[0]) + sh_ref[0]
    if dil == 1:
        h_ref[...] = h.astype(BF16)
    else:
        for cb in range(D_MODEL // LANES):
            cs = slice(cb * LANES, (cb + 1) * LANES)
            hf_ref[0][cb] = h[:, cs]
            for r in range(dil):
                h_ref[r * n:(r + 1) * n, cs] = (
                    hf_ref[0][cb, pl.ds(r, n, stride=dil), :].astype(BF16))

    hb = h_ref[...]
    for k, (w_ref, o_ref) in enumerate(zip(w_refs, o_refs)):
        for j in range(w_ref.shape[1] // tn):
            cols = slice(j * tn, (j + 1) * tn)
            res = _dot(hb, w_ref[:, cols])
            if lead_scale is not None and k == 0 and j == 0:
                res = res * lead_scale
            for r in range(dil):
                o_ref[0, r, :, cols] = res[r * n:(r + 1) * n].astype(o_ref.dtype)
    if with_dt:
        _dt_tail(hb, *dt_in, *dt_out, tm)


def _inproj(x, shift, scale, w, blocks, tm, tn, dil=1, dt_params=None, lead_scale=None):
    bsz, s, _ = x.shape
    scratch = [pltpu.VMEM((tm, D_MODEL), BF16)]
    if dil > 1:
        scratch.append(pltpu.VMEM((D_MODEL // LANES, tm, LANES), F32))
    in_specs = [
        pl.BlockSpec((1, tm, D_MODEL), lambda b, i: (b, i, 0)),
        pl.BlockSpec((1, 1, D_MODEL), lambda b, i: (b, 0, 0)),
        pl.BlockSpec((1, 1, D_MODEL), lambda b, i: (b, 0, 0)),
    ]
    w_all, layer = w
    in_specs += [pl.BlockSpec((None, D_MODEL, n), lambda b, i, blk=blk: (layer, 0, blk),
                              pipeline_mode=pl.Buffered(1)) for n, blk in blocks]
    out_specs = [pl.BlockSpec((1, dil, tm // dil, n), lambda b, i: (b, 0, i, 0))
                 for n, _ in blocks]
    out_shape = [jax.ShapeDtypeStruct((bsz, dil, s // dil, n), BF16) for n, _ in blocks]
    args = [x, shift, scale] + [w_all] * len(blocks)
    if dt_params is not None:
        assert dil == 1
        nc, cpb = s // SSM_CHUNK, tm // SSM_CHUNK
        row = pl.BlockSpec((1, LANES), lambda b, i: (0, 0))
        tok = pl.BlockSpec((1, tm, LANES), lambda b, i: (b, i, 0))
        tr = pl.BlockSpec((1, cpb, LANES, SSM_CHUNK), lambda b, i: (b, i, 0, 0))
        in_specs += [pl.BlockSpec((D_MODEL, LANES), lambda b, i: (0, 0)), row, row]
        out_specs += [tok, tok, tr, tr]
        out_shape += [jax.ShapeDtypeStruct((bsz, s, LANES), F32)] * 2
        out_shape += [jax.ShapeDtypeStruct((bsz, nc, LANES, SSM_CHUNK), F32)] * 2
        args += list(dt_params)
    return pl.pallas_call(
        functools.partial(_inproj_kernel, dil=dil, tm=tm, tn=tn, n_w=len(blocks),
                          with_dt=dt_params is not None, lead_scale=lead_scale),
        grid=(bsz, s // tm),
        in_specs=in_specs,
        out_specs=out_specs,
        out_shape=out_shape,
        scratch_shapes=scratch,
        compiler_params=_params("parallel", "parallel"),
        name=f"inproj_d{dil}",
    )(*args)


def _attn_kernel(q_ref, kp_ref, k_ref, kn_ref, vp_ref, v_ref, vn_ref,
                 o_ref, lse_ref, kcat, vcat, bias, *, tq, length, dil):
    m = pl.program_id(2)
    r = A_RADIUS
    kcat[0:r] = kp_ref[0, 0]
    kcat[r:r + tq] = k_ref[0, 0]
    kcat[r + tq:r + tq + r] = kn_ref[0, 0]
    vcat[0:r] = vp_ref[0, 0]
    vcat[r:r + tq] = v_ref[0, 0]
    vcat[r + tq:r + tq + r] = vn_ref[0, 0]

    @pl.when((pl.program_id(0) == 0) & (pl.program_id(1) == 0) & (m == 0))
    def _():
        row = lax.broadcasted_iota(jnp.int32, (A_QB, A_SPAN), 0)
        col = lax.broadcasted_iota(jnp.int32, (A_QB, A_SPAN), 1)
        adist = jnp.abs(col - r - row)
        band = adist <= r
        for t, valid in enumerate((band, band & (col >= r), band & (col < A_SPAN - r))):
            dist = jnp.where(valid, adist.astype(F32) * float(dil), MASKED_DIST)
            for h in range(A_HEADS):
                bias[t * A_HEADS + h] = -(LOG2E * 2.0 ** (-8.0 * (h + 1) / A_HEADS)) * dist

    lane = lax.broadcasted_iota(jnp.int32, (A_QB, LANES), 1)
    lo_half = lane < A_HEAD_DIM
    ones = jnp.ones((A_SPAN, LANES), BF16)

    def sub_block(j, carry):
        r0 = pl.multiple_of(j * A_QB, A_QB)
        base = m * tq + j * A_QB - r
        variant = jnp.where(base < 0, 1, jnp.where(base + A_SPAN > length, 2, 0))
        m_tile = jnp.zeros((A_QB, LANES), F32)
        z_tile = jnp.ones((A_QB, LANES), F32)
        for hp in range(A_HEADS // 2):
            cs = slice(hp * LANES, (hp + 1) * LANES)
            q_pair = q_ref[0, 0, pl.ds(r0, A_QB), cs]
            k_pair = kcat[pl.ds(r0, A_SPAN), cs]
            v_ext = jnp.concatenate([vcat[pl.ds(r0, A_SPAN), cs], ones], axis=1)
            outs = []
            for a in range(2):
                h = 2 * hp + a
                keep = lo_half if a == 0 else jnp.logical_not(lo_half)
                qm = jnp.where(keep, q_pair, jnp.zeros_like(q_pair))
                sc = _dot(qm, k_pair, _NT) + bias[variant * A_HEADS + h]
                mx = jnp.max(sc, axis=-1, keepdims=True)
                p = jnp.exp2(sc - mx)
                oe = _dot(p.astype(BF16), v_ext)
                z = oe[:, LANES:]
                outs.append(oe[:, :LANES] / z)
                m_tile = jnp.where(lane == h, mx, m_tile)
                z_tile = jnp.where(lane == h, z, z_tile)
            o_pair = jnp.where(lo_half, outs[0], outs[1])
            o_ref[0, 0, pl.ds(r0, A_QB), cs] = o_pair.astype(o_ref.dtype)
        lse_ref[0, 0, pl.ds(r0, A_QB), :] = (m_tile + jnp.log2(z_tile)) * (1.0 / LOG2E)
        return carry

    lax.fori_loop(0, tq // A_QB, sub_block, 0)


def _dilated_attention(qkv, tq):
    bsz, dil, length, _ = qkv.shape
    assert length >= 2 * A_QB, "a score tile may touch only one end of the sequence"
    tq = min(tq, length)
    nblk = length // tq
    hb = tq // A_RADIUS
    nhalo = length // A_RADIUS

    main = lambda w: pl.BlockSpec((1, 1, tq, A_WIDTH), lambda b, rr, m: (b, rr, m, w))
    prev = lambda w: pl.BlockSpec(
        (1, 1, A_RADIUS, A_WIDTH), lambda b, rr, m: (b, rr, jnp.maximum(m * hb - 1, 0), w))
    nxt = lambda w: pl.BlockSpec(
        (1, 1, A_RADIUS, A_WIDTH),
        lambda b, rr, m: (b, rr, jnp.minimum((m + 1) * hb, nhalo - 1), w))

    return pl.pallas_call(
        functools.partial(_attn_kernel, tq=tq, length=length, dil=dil),
        grid=(bsz, dil, nblk),
        in_specs=[main(0), prev(1), main(1), nxt(1), prev(2), main(2), nxt(2)],
        out_specs=[
            pl.BlockSpec((1, 1, tq, A_WIDTH), lambda b, rr, m: (b, rr, m, 0)),
            pl.BlockSpec((1, 1, tq, LANES), lambda b, rr, m: (b, rr, m, 0)),
        ],
        out_shape=[
            jax.ShapeDtypeStruct((bsz, dil, length, A_WIDTH), BF16),
            jax.ShapeDtypeStruct((bsz, dil, length, LANES), F32),
        ],
        scratch_shapes=[
            pltpu.VMEM((tq + 2 * A_RADIUS, A_WIDTH), BF16),
            pltpu.VMEM((tq + 2 * A_RADIUS, A_WIDTH), BF16),
            pltpu.VMEM((3 * A_HEADS, A_QB, A_SPAN), F32),
        ],
        compiler_params=_params("arbitrary", "arbitrary", "arbitrary"),
        name=f"dilated_attn_d{dil}",
    )(qkv, qkv, qkv, qkv, qkv, qkv, qkv)


def _amerge_kernel(o1_ref, o2_ref, o3_ref, l1_ref, l2_ref, l3_ref, gate_ref, x_ref,
                   e_ref, w_ref, gm_ref, lg_ref, lb_ref, out_ref, o_nat, l_nat, *, tm):
    for slot, (o_ref, l_ref) in enumerate(((o2_ref, l2_ref), (o3_ref, l3_ref))):
        dil = o_ref.shape[1]
        n = tm // dil
        for r in range(dil):
            o_r = o_ref[0, r].astype(F32)
            for cb in range(A_WIDTH // LANES):
                o_nat[slot, cb, pl.ds(r, n, stride=dil), :] = o_r[:, cb * LANES:(cb + 1) * LANES]
            l_nat[slot, pl.ds(r, n, stride=dil), :] = l_ref[0, r]
    natural = lambda slot: jnp.concatenate(
        [o_nat[slot, cb] for cb in range(A_WIDTH // LANES)], axis=1)
    l1, l2, l3 = l1_ref[0, 0], l_nat[0], l_nat[1]
    lmax = jnp.maximum(jnp.maximum(l1, l2), l3)
    e1, e2, e3 = jnp.exp(l1 - lmax), jnp.exp(l2 - lmax), jnp.exp(l3 - lmax)
    inv = 1.0 / (e1 + e2 + e3)
    expand = e_ref[...]
    acc = None
    for e, o in ((e1, o1_ref[0, 0].astype(F32)), (e2, natural(0)), (e3, natural(1))):
        wexp = _dot(jnp.concatenate(_split2(e * inv), axis=1), expand)
        term = wexp * o
        acc = term if acc is None else acc + term
    y = acc * _silu(gate_ref[0, 0].astype(F32))
    out = _dot(y.astype(BF16), w_ref[...])
    res = DEEPNORM_ALPHA * x_ref[0] + gm_ref[0] * out
    out_ref[0] = _layer_norm_rows(res, lg_ref[...], lb_ref[...])


def _amerge_out(os_, ls_, gate, x, w_out, gate_mod, ln_g, ln_b, tm):
    bsz, s, _ = x.shape
    expand = np.zeros((2, LANES, A_WIDTH), np.float32)
    for h in range(A_HEADS):
        expand[:, h, h * A_HEAD_DIM:(h + 1) * A_HEAD_DIM] = 1.0
    expand = jnp.asarray(expand.reshape(2 * LANES, A_WIDTH), BF16)
    tok = lambda w: pl.BlockSpec((1, tm, w), lambda b, i: (b, i, 0))

    def grp(arr, blk=0):
        dil, w = arr.shape[1], (A_WIDTH if arr.shape[3] > LANES else LANES)
        return pl.BlockSpec((1, dil, tm // dil, w), lambda b, i: (b, 0, i, blk))

    return pl.pallas_call(
        functools.partial(_amerge_kernel, tm=tm),
        grid=(bsz, s // tm),
        in_specs=[
            grp(os_[0]), grp(os_[1]), grp(os_[2]), grp(ls_[0]), grp(ls_[1]), grp(ls_[2]),
            grp(gate),
            tok(D_MODEL),
            pl.BlockSpec((2 * LANES, A_WIDTH), lambda b, i: (0, 0)),
            pl.BlockSpec((None, A_WIDTH, D_MODEL), lambda b, i: (w_out[1], 0, 0)),
            pl.BlockSpec((1, 1, D_MODEL), lambda b, i: (b, 0, 0)),
            pl.BlockSpec((1, D_MODEL), lambda b, i: (0, 0)),
            pl.BlockSpec((1, D_MODEL), lambda b, i: (0, 0)),
        ],
        out_specs=tok(D_MODEL),
        out_shape=jax.ShapeDtypeStruct((bsz, s, D_MODEL), F32),
        scratch_shapes=[
            pltpu.VMEM((2, A_WIDTH // LANES, tm, LANES), F32),
            pltpu.VMEM((2, tm, LANES), F32),
        ],
        compiler_params=_params("parallel", "parallel"),
        name="attn_merge_out",
    )(*os_, *ls_, gate, x, expand, w_out[0], gate_mod, ln_g, ln_b)


def _conv_kernel(p_ref, m_ref, n_ref, s_ref, w_ref, b_ref, o_ref, cat, *, tm, nblk):
    i = pl.program_id(1)
    hal = BF16_ROWS
    cat[0:hal] = jnp.where(i > 0, p_ref[0], jnp.zeros_like(p_ref[0]))
    cat[hal:hal + tm] = m_ref[0]
    cat[hal + tm:hal + tm + hal] = jnp.where(i < nblk - 1, n_ref[0], jnp.zeros_like(n_ref[0]))
    half = SSM_CONV // 2
    taps = [k for k in range(SSM_CONV) if k != half]

    def row_block(blk, carry):
        r0 = pl.multiple_of(blk * SSM_CHUNK, SSM_CHUNK)
        for strip in range(o_ref.shape[2] // CONV_STRIP):
            cs = slice(strip * CONV_STRIP, (strip + 1) * CONV_STRIP)
            win = cat[pl.ds(r0, SSM_CHUNK + 2 * hal), cs]
            centre = cat[pl.ds(pl.multiple_of(r0 + hal, hal), SSM_CHUNK), cs]
            acc = centre.astype(F32) * w_ref[half:half + 1, cs]
            shifted = _dot(s_ref[...], win)
            for idx, k in enumerate(taps):
                acc = acc + shifted[idx * SSM_CHUNK:(idx + 1) * SSM_CHUNK] * w_ref[k:k + 1, cs]
            o_ref[0, pl.ds(r0, SSM_CHUNK), cs] = _silu(acc + b_ref[:, cs]).astype(o_ref.dtype)
        return carry

    lax.fori_loop(0, tm // SSM_CHUNK, row_block, 0, unroll=True)


def _conv_silu(zx, conv_w, conv_b, tm):
    bsz, s, _ = zx.shape
    tc = 1024
    first = SSM_INNER // tc
    nblk = s // tm
    hb = tm // BF16_ROWS
    nh = s // BF16_ROWS
    half = SSM_CONV // 2
    win = SSM_CHUNK + 2 * BF16_ROWS
    shifts = np.zeros((SSM_CONV - 1, SSM_CHUNK, win), np.float32)
    for idx, k in enumerate(k for k in range(SSM_CONV) if k != half):
        shifts[idx, np.arange(SSM_CHUNK), np.arange(SSM_CHUNK) + BF16_ROWS + k - half] = 1.0
    shifts = jnp.asarray(shifts.reshape(-1, win), BF16)
    return pl.pallas_call(
        functools.partial(_conv_kernel, tm=tm, nblk=nblk),
        grid=(bsz, nblk, SSM_CONV_DIM // tc),
        in_specs=[
            pl.BlockSpec((1, BF16_ROWS, tc),
                         lambda b, i, j: (b, jnp.maximum(i * hb - 1, 0), first + j)),
            pl.BlockSpec((1, tm, tc), lambda b, i, j: (b, i, first + j)),
            pl.BlockSpec((1, BF16_ROWS, tc),
                         lambda b, i, j: (b, jnp.minimum((i + 1) * hb, nh - 1), first + j)),
            pl.BlockSpec(((SSM_CONV - 1) * SSM_CHUNK, win), lambda b, i, j: (0, 0)),
            pl.BlockSpec((SSM_CONV, tc), lambda b, i, j: (0, j)),
            pl.BlockSpec((1, tc), lambda b, i, j: (0, j)),
        ],
        out_specs=pl.BlockSpec((1, tm, tc), lambda b, i, j: (b, i, j)),
        out_shape=jax.ShapeDtypeStruct((bsz, s, SSM_CONV_DIM), BF16),
        scratch_shapes=[pltpu.VMEM((tm + 2 * BF16_ROWS, tc), BF16)],
        compiler_params=_params("parallel", "parallel", "parallel"),
        name="ssd_conv",
    )(zx, zx, zx, shifts, conv_w, conv_b.reshape(1, SSM_CONV_DIM))


def _ssd_kernel(*refs, reverse, cps):
    if reverse:
        (xs_ref, b_ref, c_ref, dt_ref, cum_ref, dtt_ref, cumt_ref, e_ref,
         yf_ref, z_ref, d_ref, nw_ref, y_ref, state) = refs
    else:
        (xs_ref, b_ref, c_ref, dt_ref, cum_ref, dtt_ref, cumt_ref, e_ref,
         y_ref, state) = refs
    off = SSM_HEADS if reverse else 0
    far = 0 if reverse else SSM_CHUNK - 1
    L = SSM_CHUNK

    @pl.when(pl.program_id(1) == 0)
    def _():
        state[...] = jnp.zeros_like(state)

    lane = lax.broadcasted_iota(jnp.int32, (L, LANES), 1)
    own = (lane >= off) & (lane < off + SSM_HEADS)
    lo_half = lane < SSM_HEAD_DIM
    ri = lax.broadcasted_iota(jnp.int32, (L, L), 0)
    ci = lax.broadcasted_iota(jnp.int32, (L, L), 1)
    mask = (ci >= ri) if reverse else (ci <= ri)

    def chunk(step, carry):
        cc = (cps - 1 - step) if reverse else step
        rows = pl.ds(pl.multiple_of(cc * L, L), L)
        xs = xs_ref[0, rows, :]
        dt = dt_ref[0, rows, :]
        cum = cum_ref[0, rows, :]
        dtt = dtt_ref[0, cc]
        cumt = cumt_ref[0, cc]
        tot = cum[far:far + 1, :]
        w_state = dt * jnp.exp(jnp.where(own, tot - cum, 0.0))
        e_cum = jnp.exp(jnp.where(own, cum, 0.0))
        ws_hi, ws_lo = _split2(w_state)
        ec_hi, ec_lo = _split2(e_cum)
        lhs = jnp.concatenate([jnp.concatenate([ws_hi, ws_lo], axis=1),
                               jnp.concatenate([ec_hi, ec_lo], axis=1)], axis=0)
        both = _dot(lhs, e_ref[...])
        ws_exp, ec_exp = both[:L], both[L:]
        xs_f = xs.astype(F32)
        xw = (xs_f * ws_exp).astype(BF16)
        col_e = cum * LOG2E
        row_e = (cumt - jnp.log(dtt)) * LOG2E

        y_groups = []
        for g in range(SSM_GROUPS):
            ns = slice(g * SSM_STATE, (g + 1) * SSM_STATE)
            gs = slice(g * SSM_GROUP_COLS, (g + 1) * SSM_GROUP_COLS)
            bg = b_ref[0, rows, ns]
            cg = c_ref[0, rows, ns]
            cb = _dot(cg, bg, _NT)
            st = state[g]
            y_off = _dot(cg, st.astype(BF16)) * ec_exp[:, gs]
            diag = []
            for pp in range(SSM_HEADS_PER_GROUP // 2):
                h0 = g * SSM_HEADS_PER_GROUP + 2 * pp
                lms = []
                for a in range(2):
                    col = off + h0 + a
                    diff = col_e[:, col:col + 1] - row_e[col:col + 1, :]
                    lmat = jnp.where(mask, jnp.exp2(diff) * cb, 0.0)
                    lms.append(lmat.astype(BF16))
                xp = xs[:, h0 * SSM_HEAD_DIM:(h0 + 2) * SSM_HEAD_DIM]
                zero = jnp.zeros_like(xp)
                rhs = jnp.concatenate([jnp.where(lo_half, xp, zero),
                                       jnp.where(lo_half, zero, xp)], axis=0)
                diag.append(_dot(jnp.concatenate(lms, axis=1), rhs))
            y_groups.append(jnp.concatenate(diag, axis=1) + y_off)
            state[g] = ec_exp[far:far + 1, gs] * st + _dot(bg, xw[:, gs], _TN)
        y = jnp.concatenate(y_groups, axis=1)

        if reverse:
            y = y + yf_ref[0, rows, :].astype(F32) + d_ref[...] * xs_f
            y = y * _silu(z_ref[0, rows, :].astype(F32))
            ms = jnp.mean(y * y, axis=-1, keepdims=True)
            y = y * lax.rsqrt(ms + RMS_EPS) * nw_ref[...]
        y_ref[0, rows, :] = y.astype(y_ref.dtype)
        return carry

    lax.fori_loop(0, cps, chunk, 0, unroll=2)


def _ssd_scan(xbc, dt, cum, dtt, cumt, reverse, cps, extra=None):
    bsz, s, _ = xbc.shape
    nblk = s // (cps * SSM_CHUNK)
    off = SSM_HEADS if reverse else 0
    expand = np.zeros((2, LANES, SSM_INNER), np.float32)
    for h in range(SSM_HEADS):
        expand[:, off + h, h * SSM_HEAD_DIM:(h + 1) * SSM_HEAD_DIM] = 1.0
    expand = jnp.asarray(expand.reshape(2 * LANES, SSM_INNER), BF16)
    cidx = (lambda c: nblk - 1 - c) if reverse else (lambda c: c)
    b_blk = SSM_INNER // (SSM_GROUPS * SSM_STATE)
    tok = lambda w, blk=0: pl.BlockSpec((1, cps * SSM_CHUNK, w), lambda b, c: (b, cidx(c), blk))
    tr = pl.BlockSpec((1, cps, LANES, SSM_CHUNK), lambda b, c: (b, cidx(c), 0, 0))
    in_specs = [
        tok(SSM_INNER), tok(SSM_GROUPS * SSM_STATE, b_blk), tok(SSM_GROUPS * SSM_STATE, b_blk + 1),
        tok(LANES), tok(LANES), tr, tr,
        pl.BlockSpec((2 * LANES, SSM_INNER), lambda b, c: (0, 0)),
    ]
    args = [xbc, xbc, xbc, dt, cum, dtt, cumt, expand]
    if reverse:
        yf, zx, d_exp, norm_w = extra
        row = pl.BlockSpec((1, SSM_INNER), lambda b, c: (0, 0))
        in_specs += [tok(SSM_INNER), tok(SSM_INNER), row, row]
        args += [yf, zx, d_exp, norm_w]
    return pl.pallas_call(
        functools.partial(_ssd_kernel, reverse=reverse, cps=cps),
        grid=(bsz, nblk),
        in_specs=in_specs,
        out_specs=tok(SSM_INNER),
        out_shape=jax.ShapeDtypeStruct((bsz, s, SSM_INNER), BF16),
        scratch_shapes=[pltpu.VMEM((SSM_GROUPS, SSM_STATE, SSM_GROUP_COLS), F32)],
        compiler_params=_params("parallel", "arbitrary"),
        name="ssd_bwd" if reverse else "ssd_fwd",
    )(*args)


def _out_kernel(y_ref, x_ref, w_ref, gm_ref, lg_ref, lb_ref, out_ref):
    out = _dot(y_ref[0], w_ref[...])
    res = DEEPNORM_ALPHA * x_ref[0] + gm_ref[0] * out
    out_ref[0] = _layer_norm_rows(res, lg_ref[...], lb_ref[...])


def _out_proj(y, x, w_out, gate_mod, ln_g, ln_b, tm):
    bsz, s, k = y.shape
    return pl.pallas_call(
        _out_kernel,
        grid=(bsz, s // tm),
        in_specs=[
            pl.BlockSpec((1, tm, k), lambda b, i: (b, i, 0)),
            pl.BlockSpec((1, tm, D_MODEL), lambda b, i: (b, i, 0)),
            pl.BlockSpec((None, k, D_MODEL), lambda b, i: (w_out[1], 0, 0)),
            pl.BlockSpec((1, 1, D_MODEL), lambda b, i: (b, 0, 0)),
            pl.BlockSpec((1, D_MODEL), lambda b, i: (0, 0)),
            pl.BlockSpec((1, D_MODEL), lambda b, i: (0, 0)),
        ],
        out_specs=pl.BlockSpec((1, tm, D_MODEL), lambda b, i: (b, i, 0)),
        out_shape=jax.ShapeDtypeStruct((bsz, s, D_MODEL), F32),
        compiler_params=_params("parallel", "parallel"),
        name="out_proj_ln",
    )(y, x, w_out[0], gate_mod, ln_g, ln_b)


def _layer_a(x, shift, scale, gate_mod, w_in, w_out, ln_g, ln_b):
    gw = 3 * A_WIDTH
    q_scale = LOG2E / math.sqrt(A_HEAD_DIM)
    os_, ls_ = [], []
    for g, (_, dil) in enumerate(DILATION_PAIRS):
        blocks = [(gw, g)]
        if g == 0:
            blocks.append((A_WIDTH, N_DIL * gw // A_WIDTH))
        qkv, *rest = _inproj(x, shift, scale, w_in, blocks, tm=512, tn=A_WIDTH, dil=dil,
                             lead_scale=q_scale)
        if g == 0:
            gate = rest[0]
        o, lse = _dilated_attention(qkv, tq=512)
        os_.append(o)
        ls_.append(lse)
    return _amerge_out(os_, ls_, gate, x, w_out, gate_mod, ln_g, ln_b, tm=512)


def _layer_b(x, shift, scale, gate_mod, w_in, conv_w, conv_b, dt_bias, a_log, d_skip,
             norm_w, w_out, ln_g, ln_b):
    n_main = SSM_INNER + SSM_CONV_DIM
    w_dt = jnp.zeros((D_MODEL, LANES), BF16).at[:, :2 * SSM_HEADS].set(
        w_in[0][w_in[1], :, n_main:])
    pad = lambda v: jnp.zeros((1, LANES), F32).at[0, :2 * SSM_HEADS].set(v.reshape(-1))
    zx, dt, cum, dtt, cumt = _inproj(x, shift, scale, w_in, [(n_main, 0)], tm=512, tn=1024,
                                     dt_params=(w_dt, pad(dt_bias), pad(a_log)))
    zx = zx[:, 0]
    xbc = _conv_silu(zx, conv_w, conv_b, tm=512)
    yf = _ssd_scan(xbc, dt, cum, dtt, cumt, reverse=False, cps=4)
    d_exp = jnp.repeat(d_skip.astype(F32), SSM_HEAD_DIM).reshape(1, SSM_INNER)
    yn = _ssd_scan(xbc, dt, cum, dtt, cumt, reverse=True, cps=4,
                   extra=(yf, zx, d_exp, norm_w.reshape(1, SSM_INNER).astype(F32)))
    return _out_proj(yn, x, w_out, gate_mod, ln_g, ln_b, tm=512)


def kernel(x, c, ada_w, ada_b, ln_g, ln_b, a_w_in, a_w_out, b_w_in, b_conv_w, b_conv_b,
           b_dt_bias, b_a_log, b_d, b_norm_w, b_w_out):
    bsz = x.shape[0]
    mod = _modulation(c, ada_w, ada_b)
    a_w_in, a_w_out, b_w_in, b_w_out = (w.astype(BF16) for w in (a_w_in, a_w_out, b_w_in, b_w_out))
    for i in range(DEPTH):
        m3 = mod[i].reshape(bsz, 3, 1, D_MODEL)
        shift, scale, gate_mod = m3[:, 0], m3[:, 1], m3[:, 2]
        lg = ln_g[i].reshape(1, D_MODEL)
        lb = ln_b[i].reshape(1, D_MODEL)
        j = i // 2
        if i % 2 == 0:
            x = _layer_a(x, shift, scale, gate_mod, (a_w_in, j), (a_w_out, j), lg, lb)
        else:
            x = _layer_b(x, shift, scale, gate_mod, (b_w_in, j), b_conv_w[j], b_conv_b[j],
                         b_dt_bias[j], b_a_log[j], b_d[j], b_norm_w[j], (b_w_out, j), lg, lb)
    return x
```

```python
import functools
import math

import jax
import jax.numpy as jnp
import numpy as np
from jax import lax
from jax.experimental import pallas as pl
from jax.experimental.pallas import tpu as pltpu

D_MODEL = 1024
DEPTH = 4

A_HEADS = 16
A_HEAD_DIM = 64
A_WIDTH = A_HEADS * A_HEAD_DIM
DILATION_PAIRS = ((128, 1), (512, 4), (2048, 16))
N_DIL = len(DILATION_PAIRS)
A_IN_COLS = N_DIL * 3 * A_WIDTH + A_WIDTH
A_RADIUS = 64
A_QB = 128
A_SPAN = A_QB + 2 * A_RADIUS

SSM_INNER = 2 * D_MODEL
SSM_HEAD_DIM = 64
SSM_HEADS = SSM_INNER // SSM_HEAD_DIM
SSM_STATE = 128
SSM_GROUPS = 4
SSM_CONV = 5
SSM_CHUNK = 128
SSM_CONV_DIM = SSM_INNER + 2 * SSM_GROUPS * SSM_STATE
SSM_GROUP_COLS = SSM_INNER // SSM_GROUPS
SSM_HEADS_PER_GROUP = SSM_HEADS // SSM_GROUPS

DEEPNORM_ALPHA = (2 * DEPTH) ** 0.25
LN_EPS = 1e-5
RMS_EPS = 1e-5

LANES = 128
BF16_ROWS = 16
MASKED_DIST = 1e30
LOG2E = math.log2(math.e)
CONV_STRIP = 512
VMEM_LIMIT = 48 * 1024 * 1024

F32 = jnp.float32
BF16 = jnp.bfloat16

_NT = (((1,), (1,)), ((), ()))
_TN = (((0,), (0,)), ((), ()))


def _params(*sem):
    return pltpu.CompilerParams(dimension_semantics=sem, vmem_limit_bytes=VMEM_LIMIT)


def _dot(a, b, dims=None):
    if dims is None:
        return jnp.dot(a, b, preferred_element_type=F32)
    return lax.dot_general(a, b, dims, preferred_element_type=F32)


def _split2(a):
    hi = a.astype(BF16)
    lo = (a - hi.astype(F32)).astype(BF16)
    return hi, lo


def _split3(a):
    hi = a.astype(BF16)
    r = a - hi.astype(F32)
    mid = r.astype(BF16)
    lo = (r - mid.astype(F32)).astype(BF16)
    return hi, mid, lo


def _silu(x):
    return x / (1.0 + jnp.exp2(x * (-LOG2E)))


def _layer_norm_rows(r, g, b):
    mu = jnp.mean(r, axis=-1, keepdims=True)
    d = r - mu
    var = jnp.mean(d * d, axis=-1, keepdims=True)
    return d * lax.rsqrt(var + LN_EPS) * g + b


def _mod_kernel(c_ref, w_ref, b_ref, o_ref):
    cond = _silu(c_ref[...])
    w = w_ref[0]
    c_hi, c_lo = _split2(cond)
    w_hi, w_lo = _split2(w)
    acc = _dot(c_hi, w_hi) + _dot(c_lo, w_hi) + _dot(c_hi, w_lo)
    o_ref[0] = acc + b_ref[0]


def _modulation(c, ada_w, ada_b):
    bsz = c.shape[0]
    rows = 8
    cp = jnp.zeros((rows, D_MODEL), F32).at[:bsz].set(c)
    tn = 1024
    out = pl.pallas_call(
        _mod_kernel,
        grid=(DEPTH, 3 * D_MODEL // tn),
        in_specs=[
            pl.BlockSpec((rows, D_MODEL), lambda i, j: (0, 0)),
            pl.BlockSpec((1, D_MODEL, tn), lambda i, j: (i, 0, j)),
            pl.BlockSpec((1, 1, tn), lambda i, j: (i, 0, j)),
        ],
        out_specs=pl.BlockSpec((1, rows, tn), lambda i, j: (i, 0, j)),
        out_shape=jax.ShapeDtypeStruct((DEPTH, rows, 3 * D_MODEL), F32),
        compiler_params=_params("parallel", "parallel"),
        name="ada_mod",
    )(cp, ada_w, ada_b.reshape(DEPTH, 1, 3 * D_MODEL))
    return out[:, :bsz]


def _dt_tail(hb, w_ref, bias_ref, alog_ref, dt_ref, cum_ref, dtt_ref, cumt_ref, tm):
    raw = _dot(hb, w_ref[...]) + bias_ref[...]
    dt = jnp.maximum(raw, 0.0) + jnp.log(1.0 + jnp.exp(-jnp.abs(raw)))
    a = dt * (-jnp.exp(alog_ref[...]))
    dt_ref[0] = dt
    ri = lax.broadcasted_iota(jnp.int32, (SSM_CHUNK, SSM_CHUNK), 0)
    ci = lax.broadcasted_iota(jnp.int32, (SSM_CHUNK, SSM_CHUNK), 1)
    tri = jnp.concatenate([(ci <= ri).astype(BF16), (ci >= ri).astype(BF16)], axis=0)
    lane = lax.broadcasted_iota(jnp.int32, (SSM_CHUNK, LANES), 1)
    fwd_cols = lane < SSM_HEADS
    for c in range(tm // SSM_CHUNK):
        rows = slice(c * SSM_CHUNK, (c + 1) * SSM_CHUNK)
        both = _dot(tri, jnp.concatenate(_split3(a[rows]), axis=1))
        both = both[:, :LANES] + both[:, LANES:2 * LANES] + both[:, 2 * LANES:]
        cum = jnp.where(fwd_cols, both[:SSM_CHUNK], both[SSM_CHUNK:])
        cum_ref[0, rows, :] = cum
        dtt_ref[0, c] = dt[rows].T
        cumt_ref[0, c] = cum.T


def _inproj_kernel(*refs, dil, tm, tn, n_w, with_dt, lead_scale):
    x_ref, sh_ref, sc_ref = refs[:3]
    w_refs, refs = refs[3:3 + n_w], refs[3 + n_w:]
    if with_dt:
        dt_in, refs = refs[:3], refs[3:]
    o_refs, refs = refs[:n_w], refs[n_w:]
    if with_dt:
        dt_out, refs = refs[:4], refs[4:]
    h_ref, *hf_ref = refs
    n = tm // dil
    h = x_ref[0] * (1.0 + sc_ref[0]) + sh_ref[0]
    if dil == 1:
        h_ref[...] = h.astype(BF16)
    else:
        for cb in range(D_MODEL // LANES):
            cs = slice(cb * LANES, (cb + 1) * LANES)
            hf_ref[0][cb] = h[:, cs]
            for r in range(dil):
                h_ref[r * n:(r + 1) * n, cs] = (
                    hf_ref[0][cb, pl.ds(r, n, stride=dil), :].astype(BF16))

    hb = h_ref[...]
    for k, (w_ref, o_ref) in enumerate(zip(w_refs, o_refs)):
        for j in range(w_ref.shape[1] // tn):
            cols = slice(j * tn, (j + 1) * tn)
            res = _dot(hb, w_ref[:, cols])
            if lead_scale is not None and k == 0 and j == 0:
                res = res * lead_scale
            for r in range(dil):
                o_ref[0, r, :, cols] = res[r * n:(r + 1) * n].astype(o_ref.dtype)
    if with_dt:
        _dt_tail(hb, *dt_in, *dt_out, tm)


def _inproj(x, shift, scale, w, blocks, tm, tn, dil=1, dt_params=None, lead_scale=None):
    bsz, s, _ = x.shape
    scratch = [pltpu.VMEM((tm, D_MODEL), BF16)]
    if dil > 1:
        scratch.append(pltpu.VMEM((D_MODEL // LANES, tm, LANES), F32))
    in_specs = [
        pl.BlockSpec((1, tm, D_MODEL), lambda b, i: (b, i, 0)),
        pl.BlockSpec((1, 1, D_MODEL), lambda b, i: (b, 0, 0)),
        pl.BlockSpec((1, 1, D_MODEL), lambda b, i: (b, 0, 0)),
    ]
    w_all, layer = w
    in_specs += [pl.BlockSpec((None, D_MODEL, n), lambda b, i, blk=blk: (layer, 0, blk),
                              pipeline_mode=pl.Buffered(1)) for n, blk in blocks]
    out_specs = [pl.BlockSpec((1, dil, tm // dil, n), lambda b, i: (b, 0, i, 0))
                 for n, _ in blocks]
    out_shape = [jax.ShapeDtypeStruct((bsz, dil, s // dil, n), BF16) for n, _ in blocks]
    args = [x, shift, scale] + [w_all] * len(blocks)
    if dt_params is not None:
        assert dil == 1
        nc, cpb = s // SSM_CHUNK, tm // SSM_CHUNK
        row = pl.BlockSpec((1, LANES), lambda b, i: (0, 0))
        tok = pl.BlockSpec((1, tm, LANES), lambda b, i: (b, i, 0))
        tr = pl.BlockSpec((1, cpb, LANES, SSM_CHUNK), lambda b, i: (b, i, 0, 0))
        in_specs += [pl.BlockSpec((D_MODEL, LANES), lambda b, i: (0, 0)), row, row]
        out_specs += [tok, tok, tr, tr]
        out_shape += [jax.ShapeDtypeStruct((bsz, s, LANES), F32)] * 2
        out_shape += [jax.ShapeDtypeStruct((bsz, nc, LANES, SSM_CHUNK), F32)] * 2
        args += list(dt_params)
    return pl.pallas_call(
        functools.partial(_inproj_kernel, dil=dil, tm=tm, tn=tn, n_w=len(blocks),
                          with_dt=dt_params is not None, lead_scale=lead_scale),
        grid=(bsz, s // tm),
        in_specs=in_specs,
        out_specs=out_specs,
        out_shape=out_shape,
        scratch_shapes=scratch,
        compiler_params=_params("parallel", "parallel"),
        name=f"inproj_d{dil}",
    )(*args)


def _attn_kernel(q_ref, kp_ref, k_ref, kn_ref, vp_ref, v_ref, vn_ref,
                 o_ref, lse_ref, kcat, vcat, bias, *, tq, length, dil):
    m = pl.program_id(2)
    r = A_RADIUS
    kcat[0:r] = kp_ref[0, 0]
    kcat[r:r + tq] = k_ref[0, 0]
    kcat[r + tq:r + tq + r] = kn_ref[0, 0]
    vcat[0:r] = vp_ref[0, 0]
    vcat[r:r + tq] = v_ref[0, 0]
    vcat[r + tq:r + tq + r] = vn_ref[0, 0]

    @pl.when((pl.program_id(0) == 0) & (pl.program_id(1) == 0) & (m == 0))
    def _():
        row = lax.broadcasted_iota(jnp.int32, (A_QB, A_SPAN), 0)
        col = lax.broadcasted_iota(jnp.int32, (A_QB, A_SPAN), 1)
        adist = jnp.abs(col - r - row)
        band = adist <= r
        for t, valid in enumerate((band, band & (col >= r), band & (col < A_SPAN - r))):
            dist = jnp.where(valid, adist.astype(F32) * float(dil), MASKED_DIST)
            for h in range(A_HEADS):
                bias[t * A_HEADS + h] = -(LOG2E * 2.0 ** (-8.0 * (h + 1) / A_HEADS)) * dist

    lane = lax.broadcasted_iota(jnp.int32, (A_QB, LANES), 1)
    lo_half = lane < A_HEAD_DIM
    ones = jnp.ones((A_SPAN, LANES), BF16)

    def sub_block(j, carry):
        r0 = pl.multiple_of(j * A_QB, A_QB)
        base = m * tq + j * A_QB - r
        variant = jnp.where(base < 0, 1, jnp.where(base + A_SPAN > length, 2, 0))
        m_tile = jnp.zeros((A_QB, LANES), F32)
        z_tile = jnp.ones((A_QB, LANES), F32)
        for hp in range(A_HEADS // 2):
            cs = slice(hp * LANES, (hp + 1) * LANES)
            q_pair = q_ref[0, 0, pl.ds(r0, A_QB), cs]
            k_pair = kcat[pl.ds(r0, A_SPAN), cs]
            v_ext = jnp.concatenate([vcat[pl.ds(r0, A_SPAN), cs], ones], axis=1)
            outs = []
            for a in range(2):
                h = 2 * hp + a
                keep = lo_half if a == 0 else jnp.logical_not(lo_half)
                qm = jnp.where(keep, q_pair, jnp.zeros_like(q_pair))
                sc = _dot(qm, k_pair, _NT) + bias[variant * A_HEADS + h]
                mx = jnp.max(sc, axis=-1, keepdims=True)
                p = jnp.exp2(sc - mx)
                oe = _dot(p.astype(BF16), v_ext)
                z = oe[:, LANES:]
                outs.append(oe[:, :LANES] / z)
                m_tile = jnp.where(lane == h, mx, m_tile)
                z_tile = jnp.where(lane == h, z, z_tile)
            o_pair = jnp.where(lo_half, outs[0], outs[1])
            o_ref[0, 0, pl.ds(r0, A_QB), cs] = o_pair.astype(o_ref.dtype)
        lse_ref[0, 0, pl.ds(r0, A_QB), :] = (m_tile + jnp.log2(z_tile)) * (1.0 / LOG2E)
        return carry

    lax.fori_loop(0, tq // A_QB, sub_block, 0, unroll=True)


def _dilated_attention(qkv, tq):
    bsz, dil, length, _ = qkv.shape
    assert length >= 2 * A_QB, "a score tile may touch only one end of the sequence"
    tq = min(tq, length)
    nblk = length // tq
    hb = tq // A_RADIUS
    nhalo = length // A_RADIUS

    main = lambda w: pl.BlockSpec((1, 1, tq, A_WIDTH), lambda b, rr, m: (b, rr, m, w))
    prev = lambda w: pl.BlockSpec(
        (1, 1, A_RADIUS, A_WIDTH), lambda b, rr, m: (b, rr, jnp.maximum(m * hb - 1, 0), w))
    nxt = lambda w: pl.BlockSpec(
        (1, 1, A_RADIUS, A_WIDTH),
        lambda b, rr, m: (b, rr, jnp.minimum((m + 1) * hb, nhalo - 1), w))

    return pl.pallas_call(
        functools.partial(_attn_kernel, tq=tq, length=length, dil=dil),
        grid=(bsz, dil, nblk),
        in_specs=[main(0), prev(1), main(1), nxt(1), prev(2), main(2), nxt(2)],
        out_specs=[
            pl.BlockSpec((1, 1, tq, A_WIDTH), lambda b, rr, m: (b, rr, m, 0)),
            pl.BlockSpec((1, 1, tq, LANES), lambda b, rr, m: (b, rr, m, 0)),
        ],
        out_shape=[
            jax.ShapeDtypeStruct((bsz, dil, length, A_WIDTH), BF16),
            jax.ShapeDtypeStruct((bsz, dil, length, LANES), F32),
        ],
        scratch_shapes=[
            pltpu.VMEM((tq + 2 * A_RADIUS, A_WIDTH), BF16),
            pltpu.VMEM((tq + 2 * A_RADIUS, A_WIDTH), BF16),
            pltpu.VMEM((3 * A_HEADS, A_QB, A_SPAN), F32),
        ],
        compiler_params=_params("arbitrary", "arbitrary", "arbitrary"),
        name=f"dilated_attn_d{dil}",
    )(qkv, qkv, qkv, qkv, qkv, qkv, qkv)


def _amerge_kernel(o1_ref, o2_ref, o3_ref, l1_ref, l2_ref, l3_ref, gate_ref, x_ref,
                   e_ref, w_ref, gm_ref, lg_ref, lb_ref, out_ref, o_nat, l_nat, *, tm):
    for slot, (o_ref, l_ref) in enumerate(((o2_ref, l2_ref), (o3_ref, l3_ref))):
        dil = o_ref.shape[1]
        n = tm // dil
        for r in range(dil):
            o_r = o_ref[0, r].astype(F32)
            for cb in range(A_WIDTH // LANES):
                o_nat[slot, cb, pl.ds(r, n, stride=dil), :] = o_r[:, cb * LANES:(cb + 1) * LANES]
            l_nat[slot, pl.ds(r, n, stride=dil), :] = l_ref[0, r]
    natural = lambda slot: jnp.concatenate(
        [o_nat[slot, cb] for cb in range(A_WIDTH // LANES)], axis=1)
    l1, l2, l3 = l1_ref[0, 0], l_nat[0], l_nat[1]
    lmax = jnp.maximum(jnp.maximum(l1, l2), l3)
    e1, e2, e3 = jnp.exp(l1 - lmax), jnp.exp(l2 - lmax), jnp.exp(l3 - lmax)
    inv = 1.0 / (e1 + e2 + e3)
    expand = e_ref[...]
    acc = None
    for e, o in ((e1, o1_ref[0, 0].astype(F32)), (e2, natural(0)), (e3, natural(1))):
        wexp = _dot(jnp.concatenate(_split2(e * inv), axis=1), expand)
        term = wexp * o
        acc = term if acc is None else acc + term
    y = acc * _silu(gate_ref[0, 0].astype(F32))
    out = _dot(y.astype(BF16), w_ref[...])
    res = DEEPNORM_ALPHA * x_ref[0] + gm_ref[0] * out
    out_ref[0] = _layer_norm_rows(res, lg_ref[...], lb_ref[...])


def _amerge_out(os_, ls_, gate, x, w_out, gate_mod, ln_g, ln_b, tm):
    bsz, s, _ = x.shape
    expand = np.zeros((2, LANES, A_WIDTH), np.float32)
    for h in range(A_HEADS):
        expand[:, h, h * A_HEAD_DIM:(h + 1) * A_HEAD_DIM] = 1.0
    expand = jnp.asarray(expand.reshape(2 * LANES, A_WIDTH), BF16)
    tok = lambda w: pl.BlockSpec((1, tm, w), lambda b, i: (b, i, 0))

    def grp(arr, blk=0):
        dil, w = arr.shape[1], (A_WIDTH if arr.shape[3] > LANES else LANES)
        return pl.BlockSpec((1, dil, tm // dil, w), lambda b, i: (b, 0, i, blk))

    return pl.pallas_call(
        functools.partial(_amerge_kernel, tm=tm),
        grid=(bsz, s // tm),
        in_specs=[
            grp(os_[0]), grp(os_[1]), grp(os_[2]), grp(ls_[0]), grp(ls_[1]), grp(ls_[2]),
            grp(gate),
            tok(D_MODEL),
            pl.BlockSpec((2 * LANES, A_WIDTH), lambda b, i: (0, 0)),
            pl.BlockSpec((None, A_WIDTH, D_MODEL), lambda b, i: (w_out[1], 0, 0)),
            pl.BlockSpec((1, 1, D_MODEL), lambda b, i: (b, 0, 0)),
            pl.BlockSpec((1, D_MODEL), lambda b, i: (0, 0)),
            pl.BlockSpec((1, D_MODEL), lambda b, i: (0, 0)),
        ],
        out_specs=tok(D_MODEL),
        out_shape=jax.ShapeDtypeStruct((bsz, s, D_MODEL), F32),
        scratch_shapes=[
            pltpu.VMEM((2, A_WIDTH // LANES, tm, LANES), F32),
            pltpu.VMEM((2, tm, LANES), F32),
        ],
        compiler_params=_params("parallel", "parallel"),
        name="attn_merge_out",
    )(*os_, *ls_, gate, x, expand, w_out[0], gate_mod, ln_g, ln_b)


def _conv_kernel(p_ref, m_ref, n_ref, s_ref, w_ref, b_ref, o_ref, cat, *, tm, nblk):
    i = pl.program_id(1)
    hal = BF16_ROWS
    cat[0:hal] = jnp.where(i > 0, p_ref[0], jnp.zeros_like(p_ref[0]))
    cat[hal:hal + tm] = m_ref[0]
    cat[hal + tm:hal + tm + hal] = jnp.where(i < nblk - 1, n_ref[0], jnp.zeros_like(n_ref[0]))
    half = SSM_CONV // 2
    taps = [k for k in range(SSM_CONV) if k != half]

    def row_block(blk, carry):
        r0 = pl.multiple_of(blk * SSM_CHUNK, SSM_CHUNK)
        for strip in range(o_ref.shape[2] // CONV_STRIP):
            cs = slice(strip * CONV_STRIP, (strip + 1) * CONV_STRIP)
            win = cat[pl.ds(r0, SSM_CHUNK + 2 * hal), cs]
            centre = cat[pl.ds(pl.multiple_of(r0 + hal, hal), SSM_CHUNK), cs]
            acc = centre.astype(F32) * w_ref[half:half + 1, cs]
            shifted = _dot(s_ref[...], win)
            for idx, k in enumerate(taps):
                acc = acc + shifted[idx * SSM_CHUNK:(idx + 1) * SSM_CHUNK] * w_ref[k:k + 1, cs]
            o_ref[0, pl.ds(r0, SSM_CHUNK), cs] = _silu(acc + b_ref[:, cs]).astype(o_ref.dtype)
        return carry

    lax.fori_loop(0, tm // SSM_CHUNK, row_block, 0, unroll=True)


def _conv_silu(zx, conv_w, conv_b, tm):
    bsz, s, _ = zx.shape
    tc = 1024
    first = SSM_INNER // tc
    nblk = s // tm
    hb = tm // BF16_ROWS
    nh = s // BF16_ROWS
    half = SSM_CONV // 2
    win = SSM_CHUNK + 2 * BF16_ROWS
    shifts = np.zeros((SSM_CONV - 1, SSM_CHUNK, win), np.float32)
    for idx, k in enumerate(k for k in range(SSM_CONV) if k != half):
        shifts[idx, np.arange(SSM_CHUNK), np.arange(SSM_CHUNK) + BF16_ROWS + k - half] = 1.0
    shifts = jnp.asarray(shifts.reshape(-1, win), BF16)
    return pl.pallas_call(
        functools.partial(_conv_kernel, tm=tm, nblk=nblk),
        grid=(bsz, nblk, SSM_CONV_DIM // tc),
        in_specs=[
            pl.BlockSpec((1, BF16_ROWS, tc),
                         lambda b, i, j: (b, jnp.maximum(i * hb - 1, 0), first + j)),
            pl.BlockSpec((1, tm, tc), lambda b, i, j: (b, i, first + j)),
            pl.BlockSpec((1, BF16_ROWS, tc),
                         lambda b, i, j: (b, jnp.minimum((i + 1) * hb, nh - 1), first + j)),
            pl.BlockSpec(((SSM_CONV - 1) * SSM_CHUNK, win), lambda b, i, j: (0, 0)),
            pl.BlockSpec((SSM_CONV, tc), lambda b, i, j: (0, j)),
            pl.BlockSpec((1, tc), lambda b, i, j: (0, j)),
        ],
        out_specs=pl.BlockSpec((1, tm, tc), lambda b, i, j: (b, i, j)),
        out_shape=jax.ShapeDtypeStruct((bsz, s, SSM_CONV_DIM), BF16),
        scratch_shapes=[pltpu.VMEM((tm + 2 * BF16_ROWS, tc), BF16)],
        compiler_params=_params("parallel", "parallel", "parallel"),
        name="ssd_conv",
    )(zx, zx, zx, shifts, conv_w, conv_b.reshape(1, SSM_CONV_DIM))


def _ssd_kernel(*refs, reverse, cps):
    if reverse:
        (xs_ref, b_ref, c_ref, dt_ref, cum_ref, dtt_ref, cumt_ref, e_ref,
         yf_ref, z_ref, d_ref, nw_ref, y_ref, state) = refs
    else:
        (xs_ref, b_ref, c_ref, dt_ref, cum_ref, dtt_ref, cumt_ref, e_ref,
         y_ref, state) = refs
    off = SSM_HEADS if reverse else 0
    far = 0 if reverse else SSM_CHUNK - 1
    L = SSM_CHUNK

    @pl.when(pl.program_id(1) == 0)
    def _():
        state[...] = jnp.zeros_like(state)

    lane = lax.broadcasted_iota(jnp.int32, (L, LANES), 1)
    own = (lane >= off) & (lane < off + SSM_HEADS)
    lo_half = lane < SSM_HEAD_DIM
    ri = lax.broadcasted_iota(jnp.int32, (L, L), 0)
    ci = lax.broadcasted_iota(jnp.int32, (L, L), 1)
    mask = (ci >= ri) if reverse else (ci <= ri)

    def chunk(step, carry):
        cc = (cps - 1 - step) if reverse else step
        rows = pl.ds(pl.multiple_of(cc * L, L), L)
        xs = xs_ref[0, rows, :]
        dt = dt_ref[0, rows, :]
        cum = cum_ref[0, rows, :]
        dtt = dtt_ref[0, cc]
        cumt = cumt_ref[0, cc]
        tot = cum[far:far + 1, :]
        w_state = dt * jnp.exp(jnp.where(own, tot - cum, 0.0))
        e_cum = jnp.exp(jnp.where(own, cum, 0.0))
        ws_hi, ws_lo = _split2(w_state)
        ec_hi, ec_lo = _split2(e_cum)
        lhs = jnp.concatenate([jnp.concatenate([ws_hi, ws_lo], axis=1),
                               jnp.concatenate([ec_hi, ec_lo], axis=1)], axis=0)
        both = _dot(lhs, e_ref[...])
        ws_exp, ec_exp = both[:L], both[L:]
        xs_f = xs.astype(F32)
        xw = (xs_f * ws_exp).astype(BF16)
        col_e = cum * LOG2E
        row_e = (cumt - jnp.log(dtt)) * LOG2E

        y_groups = []
        for g in range(SSM_GROUPS):
            ns = slice(g * SSM_STATE, (g + 1) * SSM_STATE)
            gs = slice(g * SSM_GROUP_COLS, (g + 1) * SSM_GROUP_COLS)
            bg = b_ref[0, rows, ns]
            cg = c_ref[0, rows, ns]
            cb = _dot(cg, bg, _NT)
            st = state[g]
            y_off = _dot(cg, st.astype(BF16)) * ec_exp[:, gs]
            diag = []
            for pp in range(SSM_HEADS_PER_GROUP // 2):
                h0 = g * SSM_HEADS_PER_GROUP + 2 * pp
                lms = []
                for a in range(2):
                    col = off + h0 + a
                    diff = col_e[:, col:col + 1] - row_e[col:col + 1, :]
                    lmat = jnp.where(mask, jnp.exp2(diff) * cb, 0.0)
                    lms.append(lmat.astype(BF16))
                xp = xs[:, h0 * SSM_HEAD_DIM:(h0 + 2) * SSM_HEAD_DIM]
                zero = jnp.zeros_like(xp)
                rhs = jnp.concatenate([jnp.where(lo_half, xp, zero),
                                       jnp.where(lo_half, zero, xp)], axis=0)
                diag.append(_dot(jnp.concatenate(lms, axis=1), rhs))
            y_groups.append(jnp.concatenate(diag, axis=1) + y_off)
            state[g] = ec_exp[far:far + 1, gs] * st + _dot(bg, xw[:, gs], _TN)
        y = jnp.concatenate(y_groups, axis=1)

        if reverse:
            y = y + yf_ref[0, rows, :].astype(F32) + d_ref[...] * xs_f
            y = y * _silu(z_ref[0, rows, :].astype(F32))
            ms = jnp.mean(y * y, axis=-1, keepdims=True)
            y = y * lax.rsqrt(ms + RMS_EPS) * nw_ref[...]
        y_ref[0, rows, :] = y.astype(y_ref.dtype)
        return carry

    lax.fori_loop(0, cps, chunk, 0, unroll=True)


def _ssd_scan(xbc, dt, cum, dtt, cumt, reverse, cps, extra=None):
    bsz, s, _ = xbc.shape
    nblk = s // (cps * SSM_CHUNK)
    off = SSM_HEADS if reverse else 0
    expand = np.zeros((2, LANES, SSM_INNER), np.float32)
    for h in range(SSM_HEADS):
        expand[:, off + h, h * SSM_HEAD_DIM:(h + 1) * SSM_HEAD_DIM] = 1.0
    expand = jnp.asarray(expand.reshape(2 * LANES, SSM_INNER), BF16)
    cidx = (lambda c: nblk - 1 - c) if reverse else (lambda c: c)
    b_blk = SSM_INNER // (SSM_GROUPS * SSM_STATE)
    tok = lambda w, blk=0: pl.BlockSpec((1, cps * SSM_CHUNK, w), lambda b, c: (b, cidx(c), blk))
    tr = pl.BlockSpec((1, cps, LANES, SSM_CHUNK), lambda b, c: (b, cidx(c), 0, 0))
    in_specs = [
        tok(SSM_INNER), tok(SSM_GROUPS * SSM_STATE, b_blk), tok(SSM_GROUPS * SSM_STATE, b_blk + 1),
        tok(LANES), tok(LANES), tr, tr,
        pl.BlockSpec((2 * LANES, SSM_INNER), lambda b, c: (0, 0)),
    ]
    args = [xbc, xbc, xbc, dt, cum, dtt, cumt, expand]
    if reverse:
        yf, zx, d_exp, norm_w = extra
        row = pl.BlockSpec((1, SSM_INNER), lambda b, c: (0, 0))
        in_specs += [tok(SSM_INNER), tok(SSM_INNER), row, row]
        args += [yf, zx, d_exp, norm_w]
    return pl.pallas_call(
        functools.partial(_ssd_kernel, reverse=reverse, cps=cps),
        grid=(bsz, nblk),
        in_specs=in_specs,
        out_specs=tok(SSM_INNER),
        out_shape=jax.ShapeDtypeStruct((bsz, s, SSM_INNER), BF16),
        scratch_shapes=[pltpu.VMEM((SSM_GROUPS, SSM_STATE, SSM_GROUP_COLS), F32)],
        compiler_params=_params("parallel", "arbitrary"),
        name="ssd_bwd" if reverse else "ssd_fwd",
    )(*args)


def _out_kernel(y_ref, x_ref, w_ref, gm_ref, lg_ref, lb_ref, out_ref):
    out = _dot(y_ref[0], w_ref[...])
    res = DEEPNORM_ALPHA * x_ref[0] + gm_ref[0] * out
    out_ref[0] = _layer_norm_rows(res, lg_ref[...], lb_ref[...])


def _out_proj(y, x, w_out, gate_mod, ln_g, ln_b, tm):
    bsz, s, k = y.shape
    return pl.pallas_call(
        _out_kernel,
        grid=(bsz, s // tm),
        in_specs=[
            pl.BlockSpec((1, tm, k), lambda b, i: (b, i, 0)),
            pl.BlockSpec((1, tm, D_MODEL), lambda b, i: (b, i, 0)),
            pl.BlockSpec((None, k, D_MODEL), lambda b, i: (w_out[1], 0, 0)),
            pl.BlockSpec((1, 1, D_MODEL), lambda b, i: (b, 0, 0)),
            pl.BlockSpec((1, D_MODEL), lambda b, i: (0, 0)),
            pl.BlockSpec((1, D_MODEL), lambda b, i: (0, 0)),
        ],
        out_specs=pl.BlockSpec((1, tm, D_MODEL), lambda b, i: (b, i, 0)),
        out_shape=jax.ShapeDtypeStruct((bsz, s, D_MODEL), F32),
        compiler_params=_params("parallel", "parallel"),
        name="out_proj_ln",
    )(y, x, w_out[0], gate_mod, ln_g, ln_b)


def _layer_a(x, shift, scale, gate_mod, w_in, w_out, ln_g, ln_b):
    gw = 3 * A_WIDTH
    q_scale = LOG2E / math.sqrt(A_HEAD_DIM)
    os_, ls_ = [], []
    for g, (_, dil) in enumerate(DILATION_PAIRS):
        blocks = [(gw, g)]
        if g == 0:
            blocks.append((A_WIDTH, N_DIL * gw // A_WIDTH))
        qkv, *rest = _inproj(x, shift, scale, w_in, blocks, tm=512, tn=A_WIDTH, dil=dil,
                             lead_scale=q_scale)
        if g == 0:
            gate = rest[0]
        o, lse = _dilated_attention(qkv, tq=512)
        os_.append(o)
        ls_.append(lse)
    return _amerge_out(os_, ls_, gate, x, w_out, gate_mod, ln_g, ln_b, tm=512)


def _layer_b(x, shift, scale, gate_mod, w_in, conv_w, conv_b, dt_bias, a_log, d_skip,
             norm_w, w_out, ln_g, ln_b):
    n_main = SSM_INNER + SSM_CONV_DIM
    w_dt = jnp.zeros((D_MODEL, LANES), BF16).at[:, :2 * SSM_HEADS].set(
        w_in[0][w_in[1], :, n_main:])
    pad = lambda v: jnp.zeros((1, LANES), F32).at[0, :2 * SSM_HEADS].set(v.reshape(-1))
    zx, dt, cum, dtt, cumt = _inproj(x, shift, scale, w_in, [(n_main, 0)], tm=512, tn=1024,
                                     dt_params=(w_dt, pad(dt_bias), pad(a_log)))
    zx = zx[:, 0]
    xbc = _conv_silu(zx, conv_w, conv_b, tm=512)
    yf = _ssd_scan(xbc, dt, cum, dtt, cumt, reverse=False, cps=4)
    d_exp = jnp.repeat(d_skip.astype(F32), SSM_HEAD_DIM).reshape(1, SSM_INNER)
    yn = _ssd_scan(xbc, dt, cum, dtt, cumt, reverse=True, cps=4,
                   extra=(yf, zx, d_exp, norm_w.reshape(1, SSM_INNER).astype(F32)))
    return _out_proj(yn, x, w_out, gate_mod, ln_g, ln_b, tm=512)


def kernel(x, c, ada_w, ada_b, ln_g, ln_b, a_w_in, a_w_out, b_w_in, b_conv_w, b_conv_b,
           b_dt_bias, b_a_log, b_d, b_norm_w, b_w_out):
    bsz = x.shape[0]
    mod = _modulation(c, ada_w, ada_b)
    a_w_in, a_w_out, b_w_in, b_w_out = (w.astype(BF16) for w in (a_w_in, a_w_out, b_w_in, b_w_out))
    for i in range(DEPTH):
        m3 = mod[i].reshape(bsz, 3, 1, D_MODEL)
        shift, scale, gate_mod = m3[:, 0], m3[:, 1], m3[:, 2]
        lg = ln_g[i].reshape(1, D_MODEL)
        lb = ln_b[i].reshape(1, D_MODEL)
        j = i // 2
        if i % 2 == 0:
            x = _layer_a(x, shift, scale, gate_mod, (a_w_in, j), (a_w_out, j), lg, lb)
        else:
            x = _layer_b(x, shift, scale, gate_mod, (b_w_in, j), b_conv_w[j], b_conv_b[j],
                         b_dt_bias[j], b_a_log[j], b_d[j], b_norm_w[j], (b_w_out, j), lg, lb)
    return x
```

```python
import functools
import math

import jax
import jax.numpy as jnp
import numpy as np
from jax import lax
from jax.experimental import pallas as pl
from jax.experimental.pallas import tpu as pltpu

D_MODEL = 1024
DEPTH = 4

A_HEADS = 16
A_HEAD_DIM = 64
A_WIDTH = A_HEADS * A_HEAD_DIM
DILATION_PAIRS = ((128, 1), (512, 4), (2048, 16))
N_DIL = len(DILATION_PAIRS)
A_IN_COLS = N_DIL * 3 * A_WIDTH + A_WIDTH
A_RADIUS = 64
A_QB = 128
A_SPAN = A_QB + 2 * A_RADIUS

SSM_INNER = 2 * D_MODEL
SSM_HEAD_DIM = 64
SSM_HEADS = SSM_INNER // SSM_HEAD_DIM
SSM_STATE = 128
SSM_GROUPS = 4
SSM_CONV = 5
SSM_CHUNK = 128
SSM_CONV_DIM = SSM_INNER + 2 * SSM_GROUPS * SSM_STATE
SSM_GROUP_COLS = SSM_INNER // SSM_GROUPS
SSM_HEADS_PER_GROUP = SSM_HEADS // SSM_GROUPS

DEEPNORM_ALPHA = (2 * DEPTH) ** 0.25
LN_EPS = 1e-5
RMS_EPS = 1e-5

LANES = 128
BF16_ROWS = 16
MASKED_DIST = 1e30
LOG2E = math.log2(math.e)
CONV_STRIP = 512
VMEM_LIMIT = 48 * 1024 * 1024

F32 = jnp.float32
BF16 = jnp.bfloat16

_NT = (((1,), (1,)), ((), ()))
_TN = (((0,), (0,)), ((), ()))


def _params(*sem):
    return pltpu.CompilerParams(dimension_semantics=sem, vmem_limit_bytes=VMEM_LIMIT)


def _dot(a, b, dims=None):
    if dims is None:
        return jnp.dot(a, b, preferred_element_type=F32)
    return lax.dot_general(a, b, dims, preferred_element_type=F32)


def _split2(a):
    hi = a.astype(BF16)
    lo = (a - hi.astype(F32)).astype(BF16)
    return hi, lo


def _split3(a):
    hi = a.astype(BF16)
    r = a - hi.astype(F32)
    mid = r.astype(BF16)
    lo = (r - mid.astype(F32)).astype(BF16)
    return hi, mid, lo


def _silu(x):
    return x / (1.0 + jnp.exp2(x * (-LOG2E)))


def _layer_norm_rows(r, g, b):
    mu = jnp.mean(r, axis=-1, keepdims=True)
    d = r - mu
    var = jnp.mean(d * d, axis=-1, keepdims=True)
    return d * lax.rsqrt(var + LN_EPS) * g + b


def _mod_kernel(c_ref, w_ref, b_ref, o_ref):
    cond = _silu(c_ref[...])
    w = w_ref[0]
    c_hi, c_lo = _split2(cond)
    w_hi, w_lo = _split2(w)
    acc = _dot(c_hi, w_hi) + _dot(c_lo, w_hi) + _dot(c_hi, w_lo)
    o_ref[0] = acc + b_ref[0]


def _modulation(c, ada_w, ada_b):
    bsz = c.shape[0]
    rows = 8
    cp = jnp.zeros((rows, D_MODEL), F32).at[:bsz].set(c)
    tn = 1024
    out = pl.pallas_call(
        _mod_kernel,
        grid=(DEPTH, 3 * D_MODEL // tn),
        in_specs=[
            pl.BlockSpec((rows, D_MODEL), lambda i, j: (0, 0)),
            pl.BlockSpec((1, D_MODEL, tn), lambda i, j: (i, 0, j)),
            pl.BlockSpec((1, 1, tn), lambda i, j: (i, 0, j)),
        ],
        out_specs=pl.BlockSpec((1, rows, tn), lambda i, j: (i, 0, j)),
        out_shape=jax.ShapeDtypeStruct((DEPTH, rows, 3 * D_MODEL), F32),
        compiler_params=_params("parallel", "parallel"),
        name="ada_mod",
    )(cp, ada_w, ada_b.reshape(DEPTH, 1, 3 * D_MODEL))
    return out[:, :bsz]


def _dt_tail(hb, w_ref, bias_ref, alog_ref, dt_ref, cum_ref, dtt_ref, cumt_ref, tm):
    raw = _dot(hb, w_ref[...]) + bias_ref[...]
    dt = jnp.maximum(raw, 0.0) + jnp.log(1.0 + jnp.exp(-jnp.abs(raw)))
    a = dt * (-jnp.exp(alog_ref[...]))
    dt_ref[0] = dt
    ri = lax.broadcasted_iota(jnp.int32, (SSM_CHUNK, SSM_CHUNK), 0)
    ci = lax.broadcasted_iota(jnp.int32, (SSM_CHUNK, SSM_CHUNK), 1)
    tri = jnp.concatenate([(ci <= ri).astype(BF16), (ci >= ri).astype(BF16)], axis=0)
    lane = lax.broadcasted_iota(jnp.int32, (SSM_CHUNK, LANES), 1)
    fwd_cols = lane < SSM_HEADS
    for c in range(tm // SSM_CHUNK):
        rows = slice(c * SSM_CHUNK, (c + 1) * SSM_CHUNK)
        both = _dot(tri, jnp.concatenate(_split3(a[rows]), axis=1))
        both = both[:, :LANES] + both[:, LANES:2 * LANES] + both[:, 2 * LANES:]
        cum = jnp.where(fwd_cols, both[:SSM_CHUNK], both[SSM_CHUNK:])
        cum_ref[0, rows, :] = cum
        dtt_ref[0, c] = dt[rows].T
        cumt_ref[0, c] = cum.T


def _inproj_kernel(*refs, dil, tm, tn, n_w, with_dt, lead_scale):
    x_ref, sh_ref, sc_ref = refs[:3]
    w_refs, refs = refs[3:3 + n_w], refs[3 + n_w:]
    if with_dt:
        dt_in, refs = refs[:3], refs[3:]
    o_refs, refs = refs[:n_w], refs[n_w:]
    if with_dt:
        dt_out, refs = refs[:4], refs[4:]
    h_ref, *hf_ref = refs
    parts = h_ref.shape[0]
    tp = tm // parts
    n = tp // dil
    for part in range(parts):
        h = x_ref[0, part * tp:(part + 1) * tp, :] * (1.0 + sc_ref[0]) + sh_ref[0]
        if dil == 1:
            h_ref[part] = h.astype(BF16)
        else:
            for cb in range(D_MODEL // LANES):
                cs = slice(cb * LANES, (cb + 1) * LANES)
                hf_ref[0][part, cb] = h[:, cs]
                for r in range(dil):
                    h_ref[part, r * n:(r + 1) * n, cs] = (
                        hf_ref[0][part, cb, pl.ds(r, n, stride=dil), :].astype(BF16))

        hb = h_ref[part]
        for k, (w_ref, o_ref) in enumerate(zip(w_refs, o_refs)):
            for j in range(w_ref.shape[1] // tn):
                cols = slice(j * tn, (j + 1) * tn)
                res = _dot(hb, w_ref[:, cols])
                if lead_scale is not None and k == 0 and j == 0:
                    res = res * lead_scale
                for r in range(dil):
                    o_ref[0, r, part * n:(part + 1) * n, cols] = (
                        res[r * n:(r + 1) * n].astype(o_ref.dtype))
    if with_dt:
        assert parts == 1
        _dt_tail(hb, *dt_in, *dt_out, tm)


def _inproj(x, shift, scale, w, blocks, tm, tn, dil=1, dt_params=None, lead_scale=None):
    bsz, s, _ = x.shape
    parts = tm // 512
    tp = tm // parts
    scratch = [pltpu.VMEM((parts, tp, D_MODEL), BF16)]
    if dil > 1:
        scratch.append(pltpu.VMEM((parts, D_MODEL // LANES, tp, LANES), F32))
    in_specs = [
        pl.BlockSpec((1, tm, D_MODEL), lambda b, i: (b, i, 0)),
        pl.BlockSpec((1, 1, D_MODEL), lambda b, i: (b, 0, 0)),
        pl.BlockSpec((1, 1, D_MODEL), lambda b, i: (b, 0, 0)),
    ]
    w_all, layer = w
    in_specs += [pl.BlockSpec((None, D_MODEL, n), lambda b, i, blk=blk: (layer, 0, blk),
                              pipeline_mode=pl.Buffered(1)) for n, blk in blocks]
    out_specs = [pl.BlockSpec((1, dil, tm // dil, n), lambda b, i: (b, 0, i, 0))
                 for n, _ in blocks]
    out_shape = [jax.ShapeDtypeStruct((bsz, dil, s // dil, n), BF16) for n, _ in blocks]
    args = [x, shift, scale] + [w_all] * len(blocks)
    if dt_params is not None:
        assert dil == 1
        nc, cpb = s // SSM_CHUNK, tm // SSM_CHUNK
        row = pl.BlockSpec((1, LANES), lambda b, i: (0, 0))
        tok = pl.BlockSpec((1, tm, LANES), lambda b, i: (b, i, 0))
        tr = pl.BlockSpec((1, cpb, LANES, SSM_CHUNK), lambda b, i: (b, i, 0, 0))
        in_specs += [pl.BlockSpec((D_MODEL, LANES), lambda b, i: (0, 0)), row, row]
        out_specs += [tok, tok, tr, tr]
        out_shape += [jax.ShapeDtypeStruct((bsz, s, LANES), F32)] * 2
        out_shape += [jax.ShapeDtypeStruct((bsz, nc, LANES, SSM_CHUNK), F32)] * 2
        args += list(dt_params)
    return pl.pallas_call(
        functools.partial(_inproj_kernel, dil=dil, tm=tm, tn=tn, n_w=len(blocks),
                          with_dt=dt_params is not None, lead_scale=lead_scale),
        grid=(bsz, s // tm),
        in_specs=in_specs,
        out_specs=out_specs,
        out_shape=out_shape,
        scratch_shapes=scratch,
        compiler_params=_params("parallel", "parallel"),
        name=f"inproj_d{dil}",
    )(*args)


def _attn_kernel(q_ref, kp_ref, k_ref, kn_ref, vp_ref, v_ref, vn_ref,
                 o_ref, lse_ref, kcat, vcat, bias, *, tq, length, dil):
    m = pl.program_id(2)
    r = A_RADIUS
    kcat[0:r] = kp_ref[0, 0]
    kcat[r:r + tq] = k_ref[0, 0]
    kcat[r + tq:r + tq + r] = kn_ref[0, 0]
    vcat[0:r] = vp_ref[0, 0]
    vcat[r:r + tq] = v_ref[0, 0]
    vcat[r + tq:r + tq + r] = vn_ref[0, 0]

    @pl.when((pl.program_id(0) == 0) & (pl.program_id(1) == 0) & (m == 0))
    def _():
        row = lax.broadcasted_iota(jnp.int32, (A_QB, A_SPAN), 0)
        col = lax.broadcasted_iota(jnp.int32, (A_QB, A_SPAN), 1)
        adist = jnp.abs(col - r - row)
        band = adist <= r
        for t, valid in enumerate((band, band & (col >= r), band & (col < A_SPAN - r))):
            dist = jnp.where(valid, adist.astype(F32) * float(dil), MASKED_DIST)
            for h in range(A_HEADS):
                bias[t * A_HEADS + h] = -(LOG2E * 2.0 ** (-8.0 * (h + 1) / A_HEADS)) * dist

    lane = lax.broadcasted_iota(jnp.int32, (A_QB, LANES), 1)
    lo_half = lane < A_HEAD_DIM
    ones = jnp.ones((A_SPAN, LANES), BF16)

    def sub_block(j, carry):
        r0 = pl.multiple_of(j * A_QB, A_QB)
        base = m * tq + j * A_QB - r
        variant = jnp.where(base < 0, 1, jnp.where(base + A_SPAN > length, 2, 0))
        m_tile = jnp.zeros((A_QB, LANES), F32)
        z_tile = jnp.ones((A_QB, LANES), F32)
        for hp in range(A_HEADS // 2):
            cs = slice(hp * LANES, (hp + 1) * LANES)
            q_pair = q_ref[0, 0, pl.ds(r0, A_QB), cs]
            k_pair = kcat[pl.ds(r0, A_SPAN), cs]
            v_ext = jnp.concatenate([vcat[pl.ds(r0, A_SPAN), cs], ones], axis=1)
            outs = []
            for a in range(2):
                h = 2 * hp + a
                keep = lo_half if a == 0 else jnp.logical_not(lo_half)
                qm = jnp.where(keep, q_pair, jnp.zeros_like(q_pair))
                sc = _dot(qm, k_pair, _NT) + bias[variant * A_HEADS + h]
                mx = jnp.max(sc, axis=-1, keepdims=True)
                p = jnp.exp2(sc - mx)
                oe = _dot(p.astype(BF16), v_ext)
                z = oe[:, LANES:]
                outs.append(oe[:, :LANES] / z)
                m_tile = jnp.where(lane == h, mx, m_tile)
                z_tile = jnp.where(lane == h, z, z_tile)
            o_pair = jnp.where(lo_half, outs[0], outs[1])
            o_ref[0, 0, pl.ds(r0, A_QB), cs] = o_pair.astype(o_ref.dtype)
        lse_ref[0, 0, pl.ds(r0, A_QB), :] = (m_tile + jnp.log2(z_tile)) * (1.0 / LOG2E)
        return carry

    lax.fori_loop(0, tq // A_QB, sub_block, 0, unroll=True)


def _dilated_attention(qkv, tq):
    bsz, dil, length, _ = qkv.shape
    assert length >= 2 * A_QB, "a score tile may touch only one end of the sequence"
    tq = min(tq, length)
    nblk = length // tq
    hb = tq // A_RADIUS
    nhalo = length // A_RADIUS

    main = lambda w: pl.BlockSpec((1, 1, tq, A_WIDTH), lambda b, rr, m: (b, rr, m, w))
    prev = lambda w: pl.BlockSpec(
        (1, 1, A_RADIUS, A_WIDTH), lambda b, rr, m: (b, rr, jnp.maximum(m * hb - 1, 0), w))
    nxt = lambda w: pl.BlockSpec(
        (1, 1, A_RADIUS, A_WIDTH),
        lambda b, rr, m: (b, rr, jnp.minimum((m + 1) * hb, nhalo - 1), w))

    return pl.pallas_call(
        functools.partial(_attn_kernel, tq=tq, length=length, dil=dil),
        grid=(bsz, dil, nblk),
        in_specs=[main(0), prev(1), main(1), nxt(1), prev(2), main(2), nxt(2)],
        out_specs=[
            pl.BlockSpec((1, 1, tq, A_WIDTH), lambda b, rr, m: (b, rr, m, 0)),
            pl.BlockSpec((1, 1, tq, LANES), lambda b, rr, m: (b, rr, m, 0)),
        ],
        out_shape=[
            jax.ShapeDtypeStruct((bsz, dil, length, A_WIDTH), BF16),
            jax.ShapeDtypeStruct((bsz, dil, length, LANES), F32),
        ],
        scratch_shapes=[
            pltpu.VMEM((tq + 2 * A_RADIUS, A_WIDTH), BF16),
            pltpu.VMEM((tq + 2 * A_RADIUS, A_WIDTH), BF16),
            pltpu.VMEM((3 * A_HEADS, A_QB, A_SPAN), F32),
        ],
        compiler_params=_params("arbitrary", "arbitrary", "arbitrary"),
        name=f"dilated_attn_d{dil}",
    )(qkv, qkv, qkv, qkv, qkv, qkv, qkv)


def _amerge_kernel(o1_ref, o2_ref, o3_ref, l1_ref, l2_ref, l3_ref, gate_ref, x_ref,
                   e_ref, w_ref, gm_ref, lg_ref, lb_ref, out_ref, o_nat, l_nat, *, tm):
    for slot, (o_ref, l_ref) in enumerate(((o2_ref, l2_ref), (o3_ref, l3_ref))):
        dil = o_ref.shape[1]
        n = tm // dil
        for r in range(dil):
            o_r = o_ref[0, r].astype(F32)
            for cb in range(A_WIDTH // LANES):
                o_nat[slot, cb, pl.ds(r, n, stride=dil), :] = o_r[:, cb * LANES:(cb + 1) * LANES]
            l_nat[slot, pl.ds(r, n, stride=dil), :] = l_ref[0, r]
    natural = lambda slot: jnp.concatenate(
        [o_nat[slot, cb] for cb in range(A_WIDTH // LANES)], axis=1)
    l1, l2, l3 = l1_ref[0, 0], l_nat[0], l_nat[1]
    lmax = jnp.maximum(jnp.maximum(l1, l2), l3)
    e1, e2, e3 = jnp.exp(l1 - lmax), jnp.exp(l2 - lmax), jnp.exp(l3 - lmax)
    inv = 1.0 / (e1 + e2 + e3)
    expand = e_ref[...]
    acc = None
    for e, o in ((e1, o1_ref[0, 0].astype(F32)), (e2, natural(0)), (e3, natural(1))):
        wexp = _dot(jnp.concatenate(_split2(e * inv), axis=1), expand)
        term = wexp * o
        acc = term if acc is None else acc + term
    y = acc * _silu(gate_ref[0, 0].astype(F32))
    out = _dot(y.astype(BF16), w_ref[...])
    res = DEEPNORM_ALPHA * x_ref[0] + gm_ref[0] * out
    out_ref[0] = _layer_norm_rows(res, lg_ref[...], lb_ref[...])


def _amerge_out(os_, ls_, gate, x, w_out, gate_mod, ln_g, ln_b, tm):
    bsz, s, _ = x.shape
    expand = np.zeros((2, LANES, A_WIDTH), np.float32)
    for h in range(A_HEADS):
        expand[:, h, h * A_HEAD_DIM:(h + 1) * A_HEAD_DIM] = 1.0
    expand = jnp.asarray(expand.reshape(2 * LANES, A_WIDTH), BF16)
    tok = lambda w: pl.BlockSpec((1, tm, w), lambda b, i: (b, i, 0))

    def grp(arr, blk=0):
        dil, w = arr.shape[1], (A_WIDTH if arr.shape[3] > LANES else LANES)
        return pl.BlockSpec((1, dil, tm // dil, w), lambda b, i: (b, 0, i, blk))

    return pl.pallas_call(
        functools.partial(_amerge_kernel, tm=tm),
        grid=(bsz, s // tm),
        in_specs=[
            grp(os_[0]), grp(os_[1]), grp(os_[2]), grp(ls_[0]), grp(ls_[1]), grp(ls_[2]),
            grp(gate),
            tok(D_MODEL),
            pl.BlockSpec((2 * LANES, A_WIDTH), lambda b, i: (0, 0)),
            pl.BlockSpec((None, A_WIDTH, D_MODEL), lambda b, i: (w_out[1], 0, 0)),
            pl.BlockSpec((1, 1, D_MODEL), lambda b, i: (b, 0, 0)),
            pl.BlockSpec((1, D_MODEL), lambda b, i: (0, 0)),
            pl.BlockSpec((1, D_MODEL), lambda b, i: (0, 0)),
        ],
        out_specs=tok(D_MODEL),
        out_shape=jax.ShapeDtypeStruct((bsz, s, D_MODEL), F32),
        scratch_shapes=[
            pltpu.VMEM((2, A_WIDTH // LANES, tm, LANES), F32),
            pltpu.VMEM((2, tm, LANES), F32),
        ],
        compiler_params=_params("parallel", "parallel"),
        name="attn_merge_out",
    )(*os_, *ls_, gate, x, expand, w_out[0], gate_mod, ln_g, ln_b)


def _conv_kernel(p_ref, m_ref, n_ref, s_ref, w_ref, b_ref, o_ref, cat, *, tm, nblk):
    i = pl.program_id(1)
    hal = BF16_ROWS
    cat[0:hal] = jnp.where(i > 0, p_ref[0], jnp.zeros_like(p_ref[0]))
    cat[hal:hal + tm] = m_ref[0]
    cat[hal + tm:hal + tm + hal] = jnp.where(i < nblk - 1, n_ref[0], jnp.zeros_like(n_ref[0]))
    half = SSM_CONV // 2
    taps = [k for k in range(SSM_CONV) if k != half]

    def row_block(blk, carry):
        r0 = pl.multiple_of(blk * SSM_CHUNK, SSM_CHUNK)
        for strip in range(o_ref.shape[2] // CONV_STRIP):
            cs = slice(strip * CONV_STRIP, (strip + 1) * CONV_STRIP)
            win = cat[pl.ds(r0, SSM_CHUNK + 2 * hal), cs]
            centre = cat[pl.ds(pl.multiple_of(r0 + hal, hal), SSM_CHUNK), cs]
            acc = centre.astype(F32) * w_ref[half:half + 1, cs]
            shifted = _dot(s_ref[...], win)
            for idx, k in enumerate(taps):
                acc = acc + shifted[idx * SSM_CHUNK:(idx + 1) * SSM_CHUNK] * w_ref[k:k + 1, cs]
            o_ref[0, pl.ds(r0, SSM_CHUNK), cs] = _silu(acc + b_ref[:, cs]).astype(o_ref.dtype)
        return carry

    lax.fori_loop(0, tm // SSM_CHUNK, row_block, 0, unroll=True)


def _conv_silu(zx, conv_w, conv_b, tm):
    bsz, s, _ = zx.shape
    tc = 1024
    first = SSM_INNER // tc
    nblk = s // tm
    hb = tm // BF16_ROWS
    nh = s // BF16_ROWS
    half = SSM_CONV // 2
    win = SSM_CHUNK + 2 * BF16_ROWS
    shifts = np.zeros((SSM_CONV - 1, SSM_CHUNK, win), np.float32)
    for idx, k in enumerate(k for k in range(SSM_CONV) if k != half):
        shifts[idx, np.arange(SSM_CHUNK), np.arange(SSM_CHUNK) + BF16_ROWS + k - half] = 1.0
    shifts = jnp.asarray(shifts.reshape(-1, win), BF16)
    return pl.pallas_call(
        functools.partial(_conv_kernel, tm=tm, nblk=nblk),
        grid=(bsz, nblk, SSM_CONV_DIM // tc),
        in_specs=[
            pl.BlockSpec((1, BF16_ROWS, tc),
                         lambda b, i, j: (b, jnp.maximum(i * hb - 1, 0), first + j)),
            pl.BlockSpec((1, tm, tc), lambda b, i, j: (b, i, first + j)),
            pl.BlockSpec((1, BF16_ROWS, tc),
                         lambda b, i, j: (b, jnp.minimum((i + 1) * hb, nh - 1), first + j)),
            pl.BlockSpec(((SSM_CONV - 1) * SSM_CHUNK, win), lambda b, i, j: (0, 0)),
            pl.BlockSpec((SSM_CONV, tc), lambda b, i, j: (0, j)),
            pl.BlockSpec((1, tc), lambda b, i, j: (0, j)),
        ],
        out_specs=pl.BlockSpec((1, tm, tc), lambda b, i, j: (b, i, j)),
        out_shape=jax.ShapeDtypeStruct((bsz, s, SSM_CONV_DIM), BF16),
        scratch_shapes=[pltpu.VMEM((tm + 2 * BF16_ROWS, tc), BF16)],
        compiler_params=_params("parallel", "parallel", "parallel"),
        name="ssd_conv",
    )(zx, zx, zx, shifts, conv_w, conv_b.reshape(1, SSM_CONV_DIM))


def _ssd_kernel(*refs, reverse, cps):
    if reverse:
        (xs_ref, b_ref, c_ref, dt_ref, cum_ref, dtt_ref, cumt_ref, e_ref,
         yf_ref, z_ref, d_ref, nw_ref, y_ref, state) = refs
    else:
        (xs_ref, b_ref, c_ref, dt_ref, cum_ref, dtt_ref, cumt_ref, e_ref,
         y_ref, state) = refs
    off = SSM_HEADS if reverse else 0
    far = 0 if reverse else SSM_CHUNK - 1
    L = SSM_CHUNK

    @pl.when(pl.program_id(1) == 0)
    def _():
        state[...] = jnp.zeros_like(state)

    lane = lax.broadcasted_iota(jnp.int32, (L, LANES), 1)
    own = (lane >= off) & (lane < off + SSM_HEADS)
    lo_half = lane < SSM_HEAD_DIM
    ri = lax.broadcasted_iota(jnp.int32, (L, L), 0)
    ci = lax.broadcasted_iota(jnp.int32, (L, L), 1)
    mask = (ci >= ri) if reverse else (ci <= ri)

    def chunk(step, carry):
        cc = (cps - 1 - step) if reverse else step
        rows = pl.ds(pl.multiple_of(cc * L, L), L)
        xs = xs_ref[0, rows, :]
        dt = dt_ref[0, rows, :]
        cum = cum_ref[0, rows, :]
        dtt = dtt_ref[0, cc]
        cumt = cumt_ref[0, cc]
        tot = cum[far:far + 1, :]
        w_state = dt * jnp.exp(jnp.where(own, tot - cum, 0.0))
        e_cum = jnp.exp(jnp.where(own, cum, 0.0))
        ws_hi, ws_lo = _split2(w_state)
        ec_hi, ec_lo = _split2(e_cum)
        lhs = jnp.concatenate([jnp.concatenate([ws_hi, ws_lo], axis=1),
                               jnp.concatenate([ec_hi, ec_lo], axis=1)], axis=0)
        both = _dot(lhs, e_ref[...])
        ws_exp, ec_exp = both[:L], both[L:]
        xs_f = xs.astype(F32)
        xw = (xs_f * ws_exp).astype(BF16)
        col_e = cum * LOG2E
        row_e = (cumt - jnp.log(dtt)) * LOG2E

        y_groups = []
        for g in range(SSM_GROUPS):
            ns = slice(g * SSM_STATE, (g + 1) * SSM_STATE)
            gs = slice(g * SSM_GROUP_COLS, (g + 1) * SSM_GROUP_COLS)
            bg = b_ref[0, rows, ns]
            cg = c_ref[0, rows, ns]
            cb = _dot(cg, bg, _NT)
            st = state[g]
            y_off = _dot(cg, st.astype(BF16)) * ec_exp[:, gs]
            diag = []
            for pp in range(SSM_HEADS_PER_GROUP // 2):
                h0 = g * SSM_HEADS_PER_GROUP + 2 * pp
                lms = []
                for a in range(2):
                    col = off + h0 + a
                    diff = col_e[:, col:col + 1] - row_e[col:col + 1, :]
                    lmat = jnp.where(mask, jnp.exp2(diff) * cb, 0.0)
                    lms.append(lmat.astype(BF16))
                xp = xs[:, h0 * SSM_HEAD_DIM:(h0 + 2) * SSM_HEAD_DIM]
                zero = jnp.zeros_like(xp)
                rhs = jnp.concatenate([jnp.where(lo_half, xp, zero),
                                       jnp.where(lo_half, zero, xp)], axis=0)
                diag.append(_dot(jnp.concatenate(lms, axis=1), rhs))
            y_groups.append(jnp.concatenate(diag, axis=1) + y_off)
            state[g] = ec_exp[far:far + 1, gs] * st + _dot(bg, xw[:, gs], _TN)
        y = jnp.concatenate(y_groups, axis=1)

        if reverse:
            y = y + yf_ref[0, rows, :].astype(F32) + d_ref[...] * xs_f
            y = y * _silu(z_ref[0, rows, :].astype(F32))
            ms = jnp.mean(y * y, axis=-1, keepdims=True)
            y = y * lax.rsqrt(ms + RMS_EPS) * nw_ref[...]
        y_ref[0, rows, :] = y.astype(y_ref.dtype)
        return carry

    lax.fori_loop(0, cps, chunk, 0, unroll=True)


def _ssd_scan(xbc, dt, cum, dtt, cumt, reverse, cps, extra=None):
    bsz, s, _ = xbc.shape
    nblk = s // (cps * SSM_CHUNK)
    off = SSM_HEADS if reverse else 0
    expand = np.zeros((2, LANES, SSM_INNER), np.float32)
    for h in range(SSM_HEADS):
        expand[:, off + h, h * SSM_HEAD_DIM:(h + 1) * SSM_HEAD_DIM] = 1.0
    expand = jnp.asarray(expand.reshape(2 * LANES, SSM_INNER), BF16)
    cidx = (lambda c: nblk - 1 - c) if reverse else (lambda c: c)
    b_blk = SSM_INNER // (SSM_GROUPS * SSM_STATE)
    tok = lambda w, blk=0: pl.BlockSpec((1, cps * SSM_CHUNK, w), lambda b, c: (b, cidx(c), blk))
    tr = pl.BlockSpec((1, cps, LANES, SSM_CHUNK), lambda b, c: (b, cidx(c), 0, 0))
    in_specs = [
        tok(SSM_INNER), tok(SSM_GROUPS * SSM_STATE, b_blk), tok(SSM_GROUPS * SSM_STATE, b_blk + 1),
        tok(LANES), tok(LANES), tr, tr,
        pl.BlockSpec((2 * LANES, SSM_INNER), lambda b, c: (0, 0)),
    ]
    args = [xbc, xbc, xbc, dt, cum, dtt, cumt, expand]
    if reverse:
        yf, zx, d_exp, norm_w = extra
        row = pl.BlockSpec((1, SSM_INNER), lambda b, c: (0, 0))
        in_specs += [tok(SSM_INNER), tok(SSM_INNER), row, row]
        args += [yf, zx, d_exp, norm_w]
    return pl.pallas_call(
        functools.partial(_ssd_kernel, reverse=reverse, cps=cps),
        grid=(bsz, nblk),
        in_specs=in_specs,
        out_specs=tok(SSM_INNER),
        out_shape=jax.ShapeDtypeStruct((bsz, s, SSM_INNER), BF16),
        scratch_shapes=[pltpu.VMEM((SSM_GROUPS, SSM_STATE, SSM_GROUP_COLS), F32)],
        compiler_params=_params("parallel", "arbitrary"),
        name="ssd_bwd" if reverse else "ssd_fwd",
    )(*args)


def _out_kernel(y_ref, x_ref, w_ref, gm_ref, lg_ref, lb_ref, out_ref, *, parts):
    rows_per = y_ref.shape[1] // parts
    for part in range(parts):
        rows = slice(part * rows_per, (part + 1) * rows_per)
        out = _dot(y_ref[0, rows, :], w_ref[...])
        res = DEEPNORM_ALPHA * x_ref[0, rows, :] + gm_ref[0] * out
        out_ref[0, rows, :] = _layer_norm_rows(res, lg_ref[...], lb_ref[...])


def _out_proj(y, x, w_out, gate_mod, ln_g, ln_b, tm):
    bsz, s, k = y.shape
    return pl.pallas_call(
        functools.partial(_out_kernel, parts=tm // 512),
        grid=(bsz, s // tm),
        in_specs=[
            pl.BlockSpec((1, tm, k), lambda b, i: (b, i, 0)),
            pl.BlockSpec((1, tm, D_MODEL), lambda b, i: (b, i, 0)),
            pl.BlockSpec((None, k, D_MODEL), lambda b, i: (w_out[1], 0, 0)),
            pl.BlockSpec((1, 1, D_MODEL), lambda b, i: (b, 0, 0)),
            pl.BlockSpec((1, D_MODEL), lambda b, i: (0, 0)),
            pl.BlockSpec((1, D_MODEL), lambda b, i: (0, 0)),
        ],
        out_specs=pl.BlockSpec((1, tm, D_MODEL), lambda b, i: (b, i, 0)),
        out_shape=jax.ShapeDtypeStruct((bsz, s, D_MODEL), F32),
        compiler_params=_params("parallel", "parallel"),
        name="out_proj_ln",
    )(y, x, w_out[0], gate_mod, ln_g, ln_b)


def _layer_a(x, shift, scale, gate_mod, w_in, w_out, ln_g, ln_b):
    gw = 3 * A_WIDTH
    q_scale = LOG2E / math.sqrt(A_HEAD_DIM)
    os_, ls_ = [], []
    for g, (_, dil) in enumerate(DILATION_PAIRS):
        blocks = [(gw, g)]
        if g == 0:
            blocks.append((A_WIDTH, N_DIL * gw // A_WIDTH))
        qkv, *rest = _inproj(x, shift, scale, w_in, blocks, tm=512 if g == 0 else 1024,
                             tn=A_WIDTH, dil=dil,
                             lead_scale=q_scale)
        if g == 0:
            gate = rest[0]
        o, lse = _dilated_attention(qkv, tq=1024)
        os_.append(o)
        ls_.append(lse)
    return _amerge_out(os_, ls_, gate, x, w_out, gate_mod, ln_g, ln_b, tm=512)


def _layer_b(x, shift, scale, gate_mod, w_in, conv_w, conv_b, dt_bias, a_log, d_skip,
             norm_w, w_out, ln_g, ln_b):
    n_main = SSM_INNER + SSM_CONV_DIM
    w_dt = jnp.zeros((D_MODEL, LANES), BF16).at[:, :2 * SSM_HEADS].set(
        w_in[0][w_in[1], :, n_main:])
    pad = lambda v: jnp.zeros((1, LANES), F32).at[0, :2 * SSM_HEADS].set(v.reshape(-1))
    zx, dt, cum, dtt, cumt = _inproj(x, shift, scale, w_in, [(n_main, 0)], tm=512, tn=1024,
                                     dt_params=(w_dt, pad(dt_bias), pad(a_log)))
    zx = zx[:, 0]
    xbc = _conv_silu(zx, conv_w, conv_b, tm=512)
    yf = _ssd_scan(xbc, dt, cum, dtt, cumt, reverse=False, cps=4)
    d_exp = jnp.repeat(d_skip.astype(F32), SSM_HEAD_DIM).reshape(1, SSM_INNER)
    yn = _ssd_scan(xbc, dt, cum, dtt, cumt, reverse=True, cps=4,
                   extra=(yf, zx, d_exp, norm_w.reshape(1, SSM_INNER).astype(F32)))
    return _out_proj(yn, x, w_out, gate_mod, ln_g, ln_b, tm=1024)


def kernel(x, c, ada_w, ada_b, ln_g, ln_b, a_w_in, a_w_out, b_w_in, b_conv_w, b_conv_b,
           b_dt_bias, b_a_log, b_d, b_norm_w, b_w_out):
    bsz = x.shape[0]
    mod = _modulation(c, ada_w, ada_b)
    a_w_in, a_w_out, b_w_in, b_w_out = (w.astype(BF16) for w in (a_w_in, a_w_out, b_w_in, b_w_out))
    for i in range(DEPTH):
        m3 = mod[i].reshape(bsz, 3, 1, D_MODEL)
        shift, scale, gate_mod = m3[:, 0], m3[:, 1], m3[:, 2]
        lg = ln_g[i].reshape(1, D_MODEL)
        lb = ln_b[i].reshape(1, D_MODEL)
        j = i // 2
        if i % 2 == 0:
            x = _layer_a(x, shift, scale, gate_mod, (a_w_in, j), (a_w_out, j), lg, lb)
        else:
            x = _layer_b(x, shift, scale, gate_mod, (b_w_in, j), b_conv_w[j], b_conv_b[j],
                         b_dt_bias[j], b_a_log[j], b_d[j], b_norm_w[j], (b_w_out, j), lg, lb)
    return x
```

```python
import functools
import math

import jax
import jax.numpy as jnp
import numpy as np
from jax import lax
from jax.experimental import pallas as pl
from jax.experimental.pallas import tpu as pltpu

D_MODEL = 1024
DEPTH = 4

A_HEADS = 16
A_HEAD_DIM = 64
A_WIDTH = A_HEADS * A_HEAD_DIM
DILATION_PAIRS = ((128, 1), (512, 4), (2048, 16))
N_DIL = len(DILATION_PAIRS)
A_IN_COLS = N_DIL * 3 * A_WIDTH + A_WIDTH
A_RADIUS = 64
A_QB = 128
A_SPAN = A_QB + 2 * A_RADIUS

SSM_INNER = 2 * D_MODEL
SSM_HEAD_DIM = 64
SSM_HEADS = SSM_INNER // SSM_HEAD_DIM
SSM_STATE = 128
SSM_GROUPS = 4
SSM_CONV = 5
SSM_CHUNK = 128
SSM_CONV_DIM = SSM_INNER + 2 * SSM_GROUPS * SSM_STATE
SSM_GROUP_COLS = SSM_INNER // SSM_GROUPS
SSM_HEADS_PER_GROUP = SSM_HEADS // SSM_GROUPS

DEEPNORM_ALPHA = (2 * DEPTH) ** 0.25
LN_EPS = 1e-5
RMS_EPS = 1e-5

LANES = 128
BF16_ROWS = 16
MASKED_DIST = 1e30
LOG2E = math.log2(math.e)
CONV_STRIP = 512
VMEM_LIMIT = 48 * 1024 * 1024

F32 = jnp.float32
BF16 = jnp.bfloat16

_NT = (((1,), (1,)), ((), ()))
_TN = (((0,), (0,)), ((), ()))


def _params(*sem):
    return pltpu.CompilerParams(dimension_semantics=sem, vmem_limit_bytes=VMEM_LIMIT)


def _dot(a, b, dims=None):
    if dims is None:
        return jnp.dot(a, b, preferred_element_type=F32)
    return lax.dot_general(a, b, dims, preferred_element_type=F32)


def _split2(a):
    hi = a.astype(BF16)
    lo = (a - hi.astype(F32)).astype(BF16)
    return hi, lo


def _split3(a):
    hi = a.astype(BF16)
    r = a - hi.astype(F32)
    mid = r.astype(BF16)
    lo = (r - mid.astype(F32)).astype(BF16)
    return hi, mid, lo


def _silu(x):
    return x / (1.0 + jnp.exp2(x * (-LOG2E)))


def _layer_norm_rows(r, g, b):
    mu = jnp.mean(r, axis=-1, keepdims=True)
    d = r - mu
    var = jnp.mean(d * d, axis=-1, keepdims=True)
    return d * lax.rsqrt(var + LN_EPS) * g + b


def _mod_kernel(c_ref, w_ref, b_ref, o_ref):
    cond = _silu(c_ref[...])
    w = w_ref[0]
    c_hi, c_lo = _split2(cond)
    w_hi, w_lo = _split2(w)
    acc = _dot(c_hi, w_hi) + _dot(c_lo, w_hi) + _dot(c_hi, w_lo)
    o_ref[0] = acc + b_ref[0]


def _modulation(c, ada_w, ada_b):
    bsz = c.shape[0]
    rows = 8
    cp = jnp.zeros((rows, D_MODEL), F32).at[:bsz].set(c)
    tn = 1024
    out = pl.pallas_call(
        _mod_kernel,
        grid=(DEPTH, 3 * D_MODEL // tn),
        in_specs=[
            pl.BlockSpec((rows, D_MODEL), lambda i, j: (0, 0)),
            pl.BlockSpec((1, D_MODEL, tn), lambda i, j: (i, 0, j)),
            pl.BlockSpec((1, 1, tn), lambda i, j: (i, 0, j)),
        ],
        out_specs=pl.BlockSpec((1, rows, tn), lambda i, j: (i, 0, j)),
        out_shape=jax.ShapeDtypeStruct((DEPTH, rows, 3 * D_MODEL), F32),
        compiler_params=_params("parallel", "parallel"),
        name="ada_mod",
    )(cp, ada_w, ada_b.reshape(DEPTH, 1, 3 * D_MODEL))
    return out[:, :bsz]


def _dt_tail(hb, w_ref, bias_ref, alog_ref, dt_ref, cum_ref, dtt_ref, cumt_ref, tm):
    raw = _dot(hb, w_ref[...]) + bias_ref[...]
    dt = jnp.maximum(raw, 0.0) + jnp.log(1.0 + jnp.exp(-jnp.abs(raw)))
    a = dt * (-jnp.exp(alog_ref[...]))
    dt_ref[0] = dt
    ri = lax.broadcasted_iota(jnp.int32, (SSM_CHUNK, SSM_CHUNK), 0)
    ci = lax.broadcasted_iota(jnp.int32, (SSM_CHUNK, SSM_CHUNK), 1)
    tri = jnp.concatenate([(ci <= ri).astype(BF16), (ci >= ri).astype(BF16)], axis=0)
    lane = lax.broadcasted_iota(jnp.int32, (SSM_CHUNK, LANES), 1)
    fwd_cols = lane < SSM_HEADS
    for c in range(tm // SSM_CHUNK):
        rows = slice(c * SSM_CHUNK, (c + 1) * SSM_CHUNK)
        both = _dot(tri, jnp.concatenate(_split3(a[rows]), axis=1))
        both = both[:, :LANES] + both[:, LANES:2 * LANES] + both[:, 2 * LANES:]
        cum = jnp.where(fwd_cols, both[:SSM_CHUNK], both[SSM_CHUNK:])
        cum_ref[0, rows, :] = cum
        dtt_ref[0, c] = dt[rows].T
        cumt_ref[0, c] = cum.T


def _inproj_kernel(*refs, dil, tm, tn, n_w, with_dt, lead_scale):
    x_ref, sh_ref, sc_ref = refs[:3]
    w_refs, refs = refs[3:3 + n_w], refs[3 + n_w:]
    if with_dt:
        dt_in, refs = refs[:3], refs[3:]
    o_refs, refs = refs[:n_w], refs[n_w:]
    if with_dt:
        dt_out, refs = refs[:4], refs[4:]
    h_ref, *hf_ref = refs
    parts = h_ref.shape[0]
    tp = tm // parts
    n = tp // dil
    for part in range(parts):
        h = x_ref[0, part * tp:(part + 1) * tp, :] * (1.0 + sc_ref[0]) + sh_ref[0]
        if dil == 1:
            h_ref[part] = h.astype(BF16)
        else:
            for cb in range(D_MODEL // LANES):
                cs = slice(cb * LANES, (cb + 1) * LANES)
                hf_ref[0][part, cb] = h[:, cs]
                for r in range(dil):
                    h_ref[part, r * n:(r + 1) * n, cs] = (
                        hf_ref[0][part, cb, pl.ds(r, n, stride=dil), :].astype(BF16))

        hb = h_ref[part]
        for k, (w_ref, o_ref) in enumerate(zip(w_refs, o_refs)):
            for j in range(w_ref.shape[1] // tn):
                cols = slice(j * tn, (j + 1) * tn)
                res = _dot(hb, w_ref[:, cols])
                if lead_scale is not None and k == 0 and j == 0:
                    res = res * lead_scale
                for r in range(dil):
                    o_ref[0, r, part * n:(part + 1) * n, cols] = (
                        res[r * n:(r + 1) * n].astype(o_ref.dtype))
    if with_dt:
        assert parts == 1
        _dt_tail(hb, *dt_in, *dt_out, tm)


def _inproj(x, shift, scale, w, blocks, tm, tn, dil=1, dt_params=None, lead_scale=None):
    bsz, s, _ = x.shape
    parts = tm // 512
    tp = tm // parts
    scratch = [pltpu.VMEM((parts, tp, D_MODEL), BF16)]
    if dil > 1:
        scratch.append(pltpu.VMEM((parts, D_MODEL // LANES, tp, LANES), F32))
    in_specs = [
        pl.BlockSpec((1, tm, D_MODEL), lambda b, i: (b, i, 0)),
        pl.BlockSpec((1, 1, D_MODEL), lambda b, i: (b, 0, 0)),
        pl.BlockSpec((1, 1, D_MODEL), lambda b, i: (b, 0, 0)),
    ]
    w_all, layer = w
    in_specs += [pl.BlockSpec((None, D_MODEL, n), lambda b, i, blk=blk: (layer, 0, blk),
                              pipeline_mode=pl.Buffered(1)) for n, blk in blocks]
    out_specs = [pl.BlockSpec((1, dil, tm // dil, n), lambda b, i: (b, 0, i, 0))
                 for n, _ in blocks]
    out_shape = [jax.ShapeDtypeStruct((bsz, dil, s // dil, n), BF16) for n, _ in blocks]
    args = [x, shift, scale] + [w_all] * len(blocks)
    if dt_params is not None:
        assert dil == 1
        nc, cpb = s // SSM_CHUNK, tm // SSM_CHUNK
        row = pl.BlockSpec((1, LANES), lambda b, i: (0, 0))
        tok = pl.BlockSpec((1, tm, LANES), lambda b, i: (b, i, 0))
        tr = pl.BlockSpec((1, cpb, LANES, SSM_CHUNK), lambda b, i: (b, i, 0, 0))
        dt_blk = dt_params[0]
        in_specs += [pl.BlockSpec((None, D_MODEL, LANES), lambda b, i: (layer, 0, dt_blk)),
                     row, row]
        out_specs += [tok, tok, tr, tr]
        out_shape += [jax.ShapeDtypeStruct((bsz, s, LANES), F32)] * 2
        out_shape += [jax.ShapeDtypeStruct((bsz, nc, LANES, SSM_CHUNK), F32)] * 2
        args += [w_all, *dt_params[1:]]
    return pl.pallas_call(
        functools.partial(_inproj_kernel, dil=dil, tm=tm, tn=tn, n_w=len(blocks),
                          with_dt=dt_params is not None, lead_scale=lead_scale),
        grid=(bsz, s // tm),
        in_specs=in_specs,
        out_specs=out_specs,
        out_shape=out_shape,
        scratch_shapes=scratch,
        compiler_params=_params("parallel", "parallel"),
        name=f"inproj_d{dil}",
    )(*args)


def _attn_kernel(q_ref, kp_ref, k_ref, kn_ref, vp_ref, v_ref, vn_ref,
                 o_ref, lse_ref, kcat, vcat, bias, *, tq, length, dil):
    m = pl.program_id(2)
    r = A_RADIUS
    kcat[0:r] = kp_ref[0, 0]
    kcat[r:r + tq] = k_ref[0, 0]
    kcat[r + tq:r + tq + r] = kn_ref[0, 0]
    vcat[0:r] = vp_ref[0, 0]
    vcat[r:r + tq] = v_ref[0, 0]
    vcat[r + tq:r + tq + r] = vn_ref[0, 0]

    @pl.when((pl.program_id(0) == 0) & (pl.program_id(1) == 0) & (m == 0))
    def _():
        row = lax.broadcasted_iota(jnp.int32, (A_QB, A_SPAN), 0)
        col = lax.broadcasted_iota(jnp.int32, (A_QB, A_SPAN), 1)
        adist = jnp.abs(col - r - row)
        band = adist <= r
        for t, valid in enumerate((band, band & (col >= r), band & (col < A_SPAN - r))):
            dist = jnp.where(valid, adist.astype(F32) * float(dil), MASKED_DIST)
            for h in range(A_HEADS):
                bias[t * A_HEADS + h] = -(LOG2E * 2.0 ** (-8.0 * (h + 1) / A_HEADS)) * dist

    lane = lax.broadcasted_iota(jnp.int32, (A_QB, LANES), 1)
    lo_half = lane < A_HEAD_DIM
    ones = jnp.ones((A_SPAN, LANES), BF16)

    def sub_block(j, carry):
        r0 = pl.multiple_of(j * A_QB, A_QB)
        base = m * tq + j * A_QB - r
        variant = jnp.where(base < 0, 1, jnp.where(base + A_SPAN > length, 2, 0))
        m_tile = jnp.zeros((A_QB, LANES), F32)
        z_tile = jnp.ones((A_QB, LANES), F32)
        for hp in range(A_HEADS // 2):
            cs = slice(hp * LANES, (hp + 1) * LANES)
            q_pair = q_ref[0, 0, pl.ds(r0, A_QB), cs]
            k_pair = kcat[pl.ds(r0, A_SPAN), cs]
            v_ext = jnp.concatenate([vcat[pl.ds(r0, A_SPAN), cs], ones], axis=1)
            outs = []
            for a in range(2):
                h = 2 * hp + a
                keep = lo_half if a == 0 else jnp.logical_not(lo_half)
                qm = jnp.where(keep, q_pair, jnp.zeros_like(q_pair))
                sc = _dot(qm, k_pair, _NT) + bias[variant * A_HEADS + h]
                mx = jnp.max(sc, axis=-1, keepdims=True)
                p = jnp.exp2(sc - mx)
                oe = _dot(p.astype(BF16), v_ext)
                z = oe[:, LANES:]
                outs.append(oe[:, :LANES] / z)
                m_tile = jnp.where(lane == h, mx, m_tile)
                z_tile = jnp.where(lane == h, z, z_tile)
            o_pair = jnp.where(lo_half, outs[0], outs[1])
            o_ref[0, 0, pl.ds(r0, A_QB), cs] = o_pair.astype(o_ref.dtype)
        lse_ref[0, 0, pl.ds(r0, A_QB), :] = (m_tile + jnp.log2(z_tile)) * (1.0 / LOG2E)
        return carry

    lax.fori_loop(0, tq // A_QB, sub_block, 0, unroll=True)


def _dilated_attention(qkv, tq):
    bsz, dil, length, _ = qkv.shape
    assert length >= 2 * A_QB, "a score tile may touch only one end of the sequence"
    tq = min(tq, length)
    nblk = length // tq
    hb = tq // A_RADIUS
    nhalo = length // A_RADIUS

    main = lambda w: pl.BlockSpec((1, 1, tq, A_WIDTH), lambda b, rr, m: (b, rr, m, w))
    prev = lambda w: pl.BlockSpec(
        (1, 1, A_RADIUS, A_WIDTH), lambda b, rr, m: (b, rr, jnp.maximum(m * hb - 1, 0), w))
    nxt = lambda w: pl.BlockSpec(
        (1, 1, A_RADIUS, A_WIDTH),
        lambda b, rr, m: (b, rr, jnp.minimum((m + 1) * hb, nhalo - 1), w))

    return pl.pallas_call(
        functools.partial(_attn_kernel, tq=tq, length=length, dil=dil),
        grid=(bsz, dil, nblk),
        in_specs=[main(0), prev(1), main(1), nxt(1), prev(2), main(2), nxt(2)],
        out_specs=[
            pl.BlockSpec((1, 1, tq, A_WIDTH), lambda b, rr, m: (b, rr, m, 0)),
            pl.BlockSpec((1, 1, tq, LANES), lambda b, rr, m: (b, rr, m, 0)),
        ],
        out_shape=[
            jax.ShapeDtypeStruct((bsz, dil, length, A_WIDTH), BF16),
            jax.ShapeDtypeStruct((bsz, dil, length, LANES), F32),
        ],
        scratch_shapes=[
            pltpu.VMEM((tq + 2 * A_RADIUS, A_WIDTH), BF16),
            pltpu.VMEM((tq + 2 * A_RADIUS, A_WIDTH), BF16),
            pltpu.VMEM((3 * A_HEADS, A_QB, A_SPAN), F32),
        ],
        compiler_params=_params("arbitrary", "arbitrary", "arbitrary"),
        name=f"dilated_attn_d{dil}",
    )(qkv, qkv, qkv, qkv, qkv, qkv, qkv)


def _amerge_kernel(o1_ref, o2_ref, o3_ref, l1_ref, l2_ref, l3_ref, gate_ref, x_ref,
                   e_ref, w_ref, gm_ref, lg_ref, lb_ref, out_ref, o_nat, l_nat, *, tm):
    for slot, (o_ref, l_ref) in enumerate(((o2_ref, l2_ref), (o3_ref, l3_ref))):
        dil = o_ref.shape[1]
        n = tm // dil
        for r in range(dil):
            o_r = o_ref[0, r].astype(F32)
            for cb in range(A_WIDTH // LANES):
                o_nat[slot, cb, pl.ds(r, n, stride=dil), :] = o_r[:, cb * LANES:(cb + 1) * LANES]
            l_nat[slot, pl.ds(r, n, stride=dil), :] = l_ref[0, r]
    natural = lambda slot: jnp.concatenate(
        [o_nat[slot, cb] for cb in range(A_WIDTH // LANES)], axis=1)
    l1, l2, l3 = l1_ref[0, 0], l_nat[0], l_nat[1]
    lmax = jnp.maximum(jnp.maximum(l1, l2), l3)
    e1, e2, e3 = jnp.exp(l1 - lmax), jnp.exp(l2 - lmax), jnp.exp(l3 - lmax)
    inv = 1.0 / (e1 + e2 + e3)
    expand = e_ref[...]
    acc = None
    for e, o in ((e1, o1_ref[0, 0].astype(F32)), (e2, natural(0)), (e3, natural(1))):
        wexp = _dot(jnp.concatenate(_split2(e * inv), axis=1), expand)
        term = wexp * o
        acc = term if acc is None else acc + term
    y = acc * _silu(gate_ref[0, 0].astype(F32))
    out = _dot(y.astype(BF16), w_ref[...])
    res = DEEPNORM_ALPHA * x_ref[0] + gm_ref[0] * out
    out_ref[0] = _layer_norm_rows(res, lg_ref[...], lb_ref[...])


def _amerge_out(os_, ls_, gate, x, w_out, gate_mod, ln_g, ln_b, tm):
    bsz, s, _ = x.shape
    expand = np.zeros((2, LANES, A_WIDTH), np.float32)
    for h in range(A_HEADS):
        expand[:, h, h * A_HEAD_DIM:(h + 1) * A_HEAD_DIM] = 1.0
    expand = jnp.asarray(expand.reshape(2 * LANES, A_WIDTH), BF16)
    tok = lambda w: pl.BlockSpec((1, tm, w), lambda b, i: (b, i, 0))

    def grp(arr, blk=0):
        dil, w = arr.shape[1], (A_WIDTH if arr.shape[3] > LANES else LANES)
        return pl.BlockSpec((1, dil, tm // dil, w), lambda b, i: (b, 0, i, blk))

    return pl.pallas_call(
        functools.partial(_amerge_kernel, tm=tm),
        grid=(bsz, s // tm),
        in_specs=[
            grp(os_[0]), grp(os_[1]), grp(os_[2]), grp(ls_[0]), grp(ls_[1]), grp(ls_[2]),
            grp(gate),
            tok(D_MODEL),
            pl.BlockSpec((2 * LANES, A_WIDTH), lambda b, i: (0, 0)),
            pl.BlockSpec((None, A_WIDTH, D_MODEL), lambda b, i: (w_out[1], 0, 0)),
            pl.BlockSpec((1, 1, D_MODEL), lambda b, i: (b, 0, 0)),
            pl.BlockSpec((1, D_MODEL), lambda b, i: (0, 0)),
            pl.BlockSpec((1, D_MODEL), lambda b, i: (0, 0)),
        ],
        out_specs=tok(D_MODEL),
        out_shape=jax.ShapeDtypeStruct((bsz, s, D_MODEL), F32),
        scratch_shapes=[
            pltpu.VMEM((2, A_WIDTH // LANES, tm, LANES), F32),
            pltpu.VMEM((2, tm, LANES), F32),
        ],
        compiler_params=_params("parallel", "parallel"),
        name="attn_merge_out",
    )(*os_, *ls_, gate, x, expand, w_out[0], gate_mod, ln_g, ln_b)


def _conv_kernel(p_ref, m_ref, n_ref, s_ref, w_ref, b_ref, o_ref, cat, *, tm, nblk):
    i = pl.program_id(1)
    hal = BF16_ROWS
    cat[0:hal] = jnp.where(i > 0, p_ref[0], jnp.zeros_like(p_ref[0]))
    cat[hal:hal + tm] = m_ref[0]
    cat[hal + tm:hal + tm + hal] = jnp.where(i < nblk - 1, n_ref[0], jnp.zeros_like(n_ref[0]))
    half = SSM_CONV // 2
    taps = [k for k in range(SSM_CONV) if k != half]

    def row_block(blk, carry):
        r0 = pl.multiple_of(blk * SSM_CHUNK, SSM_CHUNK)
        for strip in range(o_ref.shape[2] // CONV_STRIP):
            cs = slice(strip * CONV_STRIP, (strip + 1) * CONV_STRIP)
            win = cat[pl.ds(r0, SSM_CHUNK + 2 * hal), cs]
            centre = cat[pl.ds(pl.multiple_of(r0 + hal, hal), SSM_CHUNK), cs]
            acc = centre.astype(F32) * w_ref[half:half + 1, cs]
            shifted = _dot(s_ref[...], win)
            for idx, k in enumerate(taps):
                acc = acc + shifted[idx * SSM_CHUNK:(idx + 1) * SSM_CHUNK] * w_ref[k:k + 1, cs]
            o_ref[0, pl.ds(r0, SSM_CHUNK), cs] = _silu(acc + b_ref[:, cs]).astype(o_ref.dtype)
        return carry

    lax.fori_loop(0, tm // SSM_CHUNK, row_block, 0, unroll=True)


def _conv_silu(zx, conv_w, conv_b, tm):
    bsz, s, _ = zx.shape
    tc = 1024
    first = SSM_INNER // tc
    nblk = s // tm
    hb = tm // BF16_ROWS
    nh = s // BF16_ROWS
    half = SSM_CONV // 2
    win = SSM_CHUNK + 2 * BF16_ROWS
    shifts = np.zeros((SSM_CONV - 1, SSM_CHUNK, win), np.float32)
    for idx, k in enumerate(k for k in range(SSM_CONV) if k != half):
        shifts[idx, np.arange(SSM_CHUNK), np.arange(SSM_CHUNK) + BF16_ROWS + k - half] = 1.0
    shifts = jnp.asarray(shifts.reshape(-1, win), BF16)
    return pl.pallas_call(
        functools.partial(_conv_kernel, tm=tm, nblk=nblk),
        grid=(bsz, nblk, SSM_CONV_DIM // tc),
        in_specs=[
            pl.BlockSpec((1, BF16_ROWS, tc),
                         lambda b, i, j: (b, jnp.maximum(i * hb - 1, 0), first + j)),
            pl.BlockSpec((1, tm, tc), lambda b, i, j: (b, i, first + j)),
            pl.BlockSpec((1, BF16_ROWS, tc),
                         lambda b, i, j: (b, jnp.minimum((i + 1) * hb, nh - 1), first + j)),
            pl.BlockSpec(((SSM_CONV - 1) * SSM_CHUNK, win), lambda b, i, j: (0, 0)),
            pl.BlockSpec((SSM_CONV, tc), lambda b, i, j: (0, j)),
            pl.BlockSpec((1, tc), lambda b, i, j: (0, j)),
        ],
        out_specs=pl.BlockSpec((1, tm, tc), lambda b, i, j: (b, i, j)),
        out_shape=jax.ShapeDtypeStruct((bsz, s, SSM_CONV_DIM), BF16),
        scratch_shapes=[pltpu.VMEM((tm + 2 * BF16_ROWS, tc), BF16)],
        compiler_params=_params("parallel", "parallel", "parallel"),
        name="ssd_conv",
    )(zx, zx, zx, shifts, conv_w, conv_b.reshape(1, SSM_CONV_DIM))


def _ssd_kernel(*refs, reverse, cps):
    if reverse:
        (xs_ref, b_ref, c_ref, dt_ref, cum_ref, dtt_ref, cumt_ref, e_ref,
         yf_ref, z_ref, d_ref, nw_ref, y_ref, state) = refs
    else:
        (xs_ref, b_ref, c_ref, dt_ref, cum_ref, dtt_ref, cumt_ref, e_ref,
         y_ref, state) = refs
    off = SSM_HEADS if reverse else 0
    far = 0 if reverse else SSM_CHUNK - 1
    L = SSM_CHUNK

    @pl.when(pl.program_id(1) == 0)
    def _():
        state[...] = jnp.zeros_like(state)

    lane = lax.broadcasted_iota(jnp.int32, (L, LANES), 1)
    own = (lane >= off) & (lane < off + SSM_HEADS)
    lo_half = lane < SSM_HEAD_DIM
    ri = lax.broadcasted_iota(jnp.int32, (L, L), 0)
    ci = lax.broadcasted_iota(jnp.int32, (L, L), 1)
    mask = (ci >= ri) if reverse else (ci <= ri)

    def chunk(step, carry):
        cc = (cps - 1 - step) if reverse else step
        rows = pl.ds(pl.multiple_of(cc * L, L), L)
        xs = xs_ref[0, rows, :]
        dt = dt_ref[0, rows, :]
        cum = cum_ref[0, rows, :]
        dtt = dtt_ref[0, cc]
        cumt = cumt_ref[0, cc]
        tot = cum[far:far + 1, :]
        w_state = dt * jnp.exp(jnp.where(own, tot - cum, 0.0))
        e_cum = jnp.exp(jnp.where(own, cum, 0.0))
        ws_hi, ws_lo = _split2(w_state)
        ec_hi, ec_lo = _split2(e_cum)
        lhs = jnp.concatenate([jnp.concatenate([ws_hi, ws_lo], axis=1),
                               jnp.concatenate([ec_hi, ec_lo], axis=1)], axis=0)
        both = _dot(lhs, e_ref[...])
        ws_exp, ec_exp = both[:L], both[L:]
        xs_f = xs.astype(F32)
        xw = (xs_f * ws_exp).astype(BF16)
        col_e = cum * LOG2E
        row_e = (cumt - jnp.log(dtt)) * LOG2E

        y_groups = []
        for g in range(SSM_GROUPS):
            ns = slice(g * SSM_STATE, (g + 1) * SSM_STATE)
            gs = slice(g * SSM_GROUP_COLS, (g + 1) * SSM_GROUP_COLS)
            bg = b_ref[0, rows, ns]
            cg = c_ref[0, rows, ns]
            cb = _dot(cg, bg, _NT)
            st = state[g]
            y_off = _dot(cg, st.astype(BF16)) * ec_exp[:, gs]
            diag = []
            for pp in range(SSM_HEADS_PER_GROUP // 2):
                h0 = g * SSM_HEADS_PER_GROUP + 2 * pp
                lms = []
                for a in range(2):
                    col = off + h0 + a
                    diff = col_e[:, col:col + 1] - row_e[col:col + 1, :]
                    lmat = jnp.where(mask, jnp.exp2(diff) * cb, 0.0)
                    lms.append(lmat.astype(BF16))
                xp = xs[:, h0 * SSM_HEAD_DIM:(h0 + 2) * SSM_HEAD_DIM]
                zero = jnp.zeros_like(xp)
                rhs = jnp.concatenate([jnp.where(lo_half, xp, zero),
                                       jnp.where(lo_half, zero, xp)], axis=0)
                diag.append(_dot(jnp.concatenate(lms, axis=1), rhs))
            y_groups.append(jnp.concatenate(diag, axis=1) + y_off)
            state[g] = ec_exp[far:far + 1, gs] * st + _dot(bg, xw[:, gs], _TN)
        y = jnp.concatenate(y_groups, axis=1)

        if reverse:
            y = y + yf_ref[0, rows, :].astype(F32) + d_ref[...] * xs_f
            y = y * _silu(z_ref[0, rows, :].astype(F32))
            ms = jnp.mean(y * y, axis=-1, keepdims=True)
            y = y * lax.rsqrt(ms + RMS_EPS) * nw_ref[...]
        y_ref[0, rows, :] = y.astype(y_ref.dtype)
        return carry

    lax.fori_loop(0, cps, chunk, 0, unroll=True)


def _ssd_scan(xbc, dt, cum, dtt, cumt, reverse, cps, extra=None):
    bsz, s, _ = xbc.shape
    nblk = s // (cps * SSM_CHUNK)
    off = SSM_HEADS if reverse else 0
    expand = np.zeros((2, LANES, SSM_INNER), np.float32)
    for h in range(SSM_HEADS):
        expand[:, off + h, h * SSM_HEAD_DIM:(h + 1) * SSM_HEAD_DIM] = 1.0
    expand = jnp.asarray(expand.reshape(2 * LANES, SSM_INNER), BF16)
    cidx = (lambda c: nblk - 1 - c) if reverse else (lambda c: c)
    b_blk = SSM_INNER // (SSM_GROUPS * SSM_STATE)
    tok = lambda w, blk=0: pl.BlockSpec((1, cps * SSM_CHUNK, w), lambda b, c: (b, cidx(c), blk))
    tr = pl.BlockSpec((1, cps, LANES, SSM_CHUNK), lambda b, c: (b, cidx(c), 0, 0))
    in_specs = [
        tok(SSM_INNER), tok(SSM_GROUPS * SSM_STATE, b_blk), tok(SSM_GROUPS * SSM_STATE, b_blk + 1),
        tok(LANES), tok(LANES), tr, tr,
        pl.BlockSpec((2 * LANES, SSM_INNER), lambda b, c: (0, 0)),
    ]
    args = [xbc, xbc, xbc, dt, cum, dtt, cumt, expand]
    if reverse:
        yf, zx, d_exp, norm_w = extra
        row = pl.BlockSpec((1, SSM_INNER), lambda b, c: (0, 0))
        in_specs += [tok(SSM_INNER), tok(SSM_INNER), row, row]
        args += [yf, zx, d_exp, norm_w]
    return pl.pallas_call(
        functools.partial(_ssd_kernel, reverse=reverse, cps=cps),
        grid=(bsz, nblk),
        in_specs=in_specs,
        out_specs=tok(SSM_INNER),
        out_shape=jax.ShapeDtypeStruct((bsz, s, SSM_INNER), BF16),
        scratch_shapes=[pltpu.VMEM((SSM_GROUPS, SSM_STATE, SSM_GROUP_COLS), F32)],
        compiler_params=_params("parallel", "arbitrary"),
        name="ssd_bwd" if reverse else "ssd_fwd",
    )(*args)


def _out_kernel(y_ref, x_ref, w_ref, gm_ref, lg_ref, lb_ref, out_ref, *, parts):
    rows_per = y_ref.shape[1] // parts
    for part in range(parts):
        rows = slice(part * rows_per, (part + 1) * rows_per)
        out = _dot(y_ref[0, rows, :], w_ref[...])
        res = DEEPNORM_ALPHA * x_ref[0, rows, :] + gm_ref[0] * out
        out_ref[0, rows, :] = _layer_norm_rows(res, lg_ref[...], lb_ref[...])


def _out_proj(y, x, w_out, gate_mod, ln_g, ln_b, tm):
    bsz, s, k = y.shape
    return pl.pallas_call(
        functools.partial(_out_kernel, parts=tm // 512),
        grid=(bsz, s // tm),
        in_specs=[
            pl.BlockSpec((1, tm, k), lambda b, i: (b, i, 0)),
            pl.BlockSpec((1, tm, D_MODEL), lambda b, i: (b, i, 0)),
            pl.BlockSpec((None, k, D_MODEL), lambda b, i: (w_out[1], 0, 0)),
            pl.BlockSpec((1, 1, D_MODEL), lambda b, i: (b, 0, 0)),
            pl.BlockSpec((1, D_MODEL), lambda b, i: (0, 0)),
            pl.BlockSpec((1, D_MODEL), lambda b, i: (0, 0)),
        ],
        out_specs=pl.BlockSpec((1, tm, D_MODEL), lambda b, i: (b, i, 0)),
        out_shape=jax.ShapeDtypeStruct((bsz, s, D_MODEL), F32),
        compiler_params=_params("parallel", "parallel"),
        name="out_proj_ln",
    )(y, x, w_out[0], gate_mod, ln_g, ln_b)


def _layer_a(x, shift, scale, gate_mod, w_in, w_out, ln_g, ln_b):
    gw = 3 * A_WIDTH
    q_scale = LOG2E / math.sqrt(A_HEAD_DIM)
    os_, ls_ = [], []
    for g, (_, dil) in enumerate(DILATION_PAIRS):
        blocks = [(gw, g)]
        if g == 0:
            blocks.append((A_WIDTH, N_DIL * gw // A_WIDTH))
        qkv, *rest = _inproj(x, shift, scale, w_in, blocks, tm=512 if g == 0 else 1024,
                             tn=A_WIDTH, dil=dil,
                             lead_scale=q_scale)
        if g == 0:
            gate = rest[0]
        o, lse = _dilated_attention(qkv, tq=1024)
        os_.append(o)
        ls_.append(lse)
    return _amerge_out(os_, ls_, gate, x, w_out, gate_mod, ln_g, ln_b, tm=512)


def _layer_b(x, shift, scale, gate_mod, w_in, conv_w, conv_b, dt_bias, a_log, d_skip,
             norm_w, w_out, ln_g, ln_b):
    n_main = SSM_INNER + SSM_CONV_DIM
    pad = lambda v: jnp.zeros((1, LANES), F32).at[0, :2 * SSM_HEADS].set(v.reshape(-1))
    zx, dt, cum, dtt, cumt = _inproj(x, shift, scale, w_in, [(n_main, 0)], tm=512, tn=1024,
                                     dt_params=(n_main // LANES, pad(dt_bias), pad(a_log)))
    zx = zx[:, 0]
    xbc = _conv_silu(zx, conv_w, conv_b, tm=1024)
    yf = _ssd_scan(xbc, dt, cum, dtt, cumt, reverse=False, cps=4)
    d_exp = jnp.repeat(d_skip.astype(F32), SSM_HEAD_DIM).reshape(1, SSM_INNER)
    yn = _ssd_scan(xbc, dt, cum, dtt, cumt, reverse=True, cps=4,
                   extra=(yf, zx, d_exp, norm_w.reshape(1, SSM_INNER).astype(F32)))
    return _out_proj(yn, x, w_out, gate_mod, ln_g, ln_b, tm=1024)


def kernel(x, c, ada_w, ada_b, ln_g, ln_b, a_w_in, a_w_out, b_w_in, b_conv_w, b_conv_b,
           b_dt_bias, b_a_log, b_d, b_norm_w, b_w_out):
    bsz = x.shape[0]
    mod = _modulation(c, ada_w, ada_b)
    a_w_in, a_w_out, b_w_out = (w.astype(BF16) for w in (a_w_in, a_w_out, b_w_out))
    lane_pad = -b_w_in.shape[2] % LANES
    b_w_in = jnp.pad(b_w_in, ((0, 0), (0, 0), (0, lane_pad))).astype(BF16)
    for i in range(DEPTH):
        m3 = mod[i].reshape(bsz, 3, 1, D_MODEL)
        shift, scale, gate_mod = m3[:, 0], m3[:, 1], m3[:, 2]
        lg = ln_g[i].reshape(1, D_MODEL)
        lb = ln_b[i].reshape(1, D_MODEL)
        j = i // 2
        if i % 2 == 0:
            x = _layer_a(x, shift, scale, gate_mod, (a_w_in, j), (a_w_out, j), lg, lb)
        else:
            x = _layer_b(x, shift, scale, gate_mod, (b_w_in, j), b_conv_w[j], b_conv_b[j],
                         b_dt_bias[j], b_a_log[j], b_d[j], b_norm_w[j], (b_w_out, j), lg, lb)
    return x
```

```python
import functools
import math

import jax
import jax.numpy as jnp
import numpy as np
from jax import lax
from jax.experimental import pallas as pl
from jax.experimental.pallas import tpu as pltpu

D_MODEL = 1024
DEPTH = 4

A_HEADS = 16
A_HEAD_DIM = 64
A_WIDTH = A_HEADS * A_HEAD_DIM
DILATION_PAIRS = ((128, 1), (512, 4), (2048, 16))
N_DIL = len(DILATION_PAIRS)
A_IN_COLS = N_DIL * 3 * A_WIDTH + A_WIDTH
A_RADIUS = 64
A_QB = 128
A_SPAN = A_QB + 2 * A_RADIUS

SSM_INNER = 2 * D_MODEL
SSM_HEAD_DIM = 64
SSM_HEADS = SSM_INNER // SSM_HEAD_DIM
SSM_STATE = 128
SSM_GROUPS = 4
SSM_CONV = 5
SSM_CHUNK = 128
SSM_CONV_DIM = SSM_INNER + 2 * SSM_GROUPS * SSM_STATE
SSM_GROUP_COLS = SSM_INNER // SSM_GROUPS
SSM_HEADS_PER_GROUP = SSM_HEADS // SSM_GROUPS

DEEPNORM_ALPHA = (2 * DEPTH) ** 0.25
LN_EPS = 1e-5
RMS_EPS = 1e-5

LANES = 128
BF16_ROWS = 16
MASKED_DIST = 1e30
LOG2E = math.log2(math.e)
CONV_STRIP = 512
VMEM_LIMIT = 48 * 1024 * 1024

F32 = jnp.float32
BF16 = jnp.bfloat16

_NT = (((1,), (1,)), ((), ()))
_TN = (((0,), (0,)), ((), ()))


def _params(*sem):
    return pltpu.CompilerParams(dimension_semantics=sem, vmem_limit_bytes=VMEM_LIMIT)


def _dot(a, b, dims=None):
    if dims is None:
        return jnp.dot(a, b, preferred_element_type=F32)
    return lax.dot_general(a, b, dims, preferred_element_type=F32)


def _split2(a):
    hi = a.astype(BF16)
    lo = (a - hi.astype(F32)).astype(BF16)
    return hi, lo


def _split3(a):
    hi = a.astype(BF16)
    r = a - hi.astype(F32)
    mid = r.astype(BF16)
    lo = (r - mid.astype(F32)).astype(BF16)
    return hi, mid, lo


def _silu(x):
    return x / (1.0 + jnp.exp2(x * (-LOG2E)))


def _layer_norm_rows(r, g, b):
    mu = jnp.mean(r, axis=-1, keepdims=True)
    d = r - mu
    var = jnp.mean(d * d, axis=-1, keepdims=True)
    return d * lax.rsqrt(var + LN_EPS) * g + b


def _mod_kernel(c_ref, w_ref, b_ref, o_ref):
    cond = _silu(c_ref[...])
    w = w_ref[0]
    c_hi, c_lo = _split2(cond)
    w_hi, w_lo = _split2(w)
    acc = _dot(c_hi, w_hi) + _dot(c_lo, w_hi) + _dot(c_hi, w_lo)
    o_ref[0] = acc + b_ref[0]


def _modulation(c, ada_w, ada_b):
    bsz = c.shape[0]
    rows = 8
    cp = jnp.zeros((rows, D_MODEL), F32).at[:bsz].set(c)
    tn = 1024
    out = pl.pallas_call(
        _mod_kernel,
        grid=(DEPTH, 3 * D_MODEL // tn),
        in_specs=[
            pl.BlockSpec((rows, D_MODEL), lambda i, j: (0, 0)),
            pl.BlockSpec((1, D_MODEL, tn), lambda i, j: (i, 0, j)),
            pl.BlockSpec((1, 1, tn), lambda i, j: (i, 0, j)),
        ],
        out_specs=pl.BlockSpec((1, rows, tn), lambda i, j: (i, 0, j)),
        out_shape=jax.ShapeDtypeStruct((DEPTH, rows, 3 * D_MODEL), F32),
        compiler_params=_params("parallel", "parallel"),
        name="ada_mod",
    )(cp, ada_w, ada_b.reshape(DEPTH, 1, 3 * D_MODEL))
    return out[:, :bsz]


def _dt_tail(hb, w_ref, bias_ref, alog_ref, dt_ref, cum_ref, dtt_ref, cumt_ref, tm):
    raw = _dot(hb, w_ref[...]) + bias_ref[...]
    dt = jnp.maximum(raw, 0.0) + jnp.log(1.0 + jnp.exp(-jnp.abs(raw)))
    a = dt * (-jnp.exp(alog_ref[...]))
    dt_ref[0] = dt
    ri = lax.broadcasted_iota(jnp.int32, (SSM_CHUNK, SSM_CHUNK), 0)
    ci = lax.broadcasted_iota(jnp.int32, (SSM_CHUNK, SSM_CHUNK), 1)
    tri = jnp.concatenate([(ci <= ri).astype(BF16), (ci >= ri).astype(BF16)], axis=0)
    lane = lax.broadcasted_iota(jnp.int32, (SSM_CHUNK, LANES), 1)
    fwd_cols = lane < SSM_HEADS
    for c in range(tm // SSM_CHUNK):
        rows = slice(c * SSM_CHUNK, (c + 1) * SSM_CHUNK)
        both = _dot(tri, jnp.concatenate(_split3(a[rows]), axis=1))
        both = both[:, :LANES] + both[:, LANES:2 * LANES] + both[:, 2 * LANES:]
        cum = jnp.where(fwd_cols, both[:SSM_CHUNK], both[SSM_CHUNK:])
        cum_ref[0, rows, :] = cum
        dtt_ref[0, c] = dt[rows].T
        cumt_ref[0, c] = cum.T


def _inproj_kernel(*refs, dil, tm, tn, n_w, with_dt, lead_scale):
    x_ref, sh_ref, sc_ref = refs[:3]
    w_refs, refs = refs[3:3 + n_w], refs[3 + n_w:]
    if with_dt:
        dt_in, refs = refs[:3], refs[3:]
    o_refs, refs = refs[:n_w], refs[n_w:]
    if with_dt:
        dt_out, refs = refs[:4], refs[4:]
    h_ref, *hf_ref = refs
    parts = h_ref.shape[0]
    tp = tm // parts
    n = tp // dil
    for part in range(parts):
        h = x_ref[0, part * tp:(part + 1) * tp, :] * (1.0 + sc_ref[0]) + sh_ref[0]
        if dil == 1:
            h_ref[part] = h.astype(BF16)
        else:
            for cb in range(D_MODEL // LANES):
                cs = slice(cb * LANES, (cb + 1) * LANES)
                hf_ref[0][part, cb] = h[:, cs]
                for r in range(dil):
                    h_ref[part, r * n:(r + 1) * n, cs] = (
                        hf_ref[0][part, cb, pl.ds(r, n, stride=dil), :].astype(BF16))

        hb = h_ref[part]
        for k, (w_ref, o_ref) in enumerate(zip(w_refs, o_refs)):
            for j in range(w_ref.shape[1] // tn):
                cols = slice(j * tn, (j + 1) * tn)
                res = _dot(hb, w_ref[:, cols])
                if lead_scale is not None and k == 0 and j == 0:
                    res = res * lead_scale
                for r in range(dil):
                    o_ref[0, r, part * n:(part + 1) * n, cols] = (
                        res[r * n:(r + 1) * n].astype(o_ref.dtype))
    if with_dt:
        assert parts == 1
        _dt_tail(hb, *dt_in, *dt_out, tm)


def _inproj(x, shift, scale, w, blocks, tm, tn, dil=1, dt_params=None, lead_scale=None):
    bsz, s, _ = x.shape
    parts = tm // 512
    tp = tm // parts
    scratch = [pltpu.VMEM((parts, tp, D_MODEL), BF16)]
    if dil > 1:
        scratch.append(pltpu.VMEM((parts, D_MODEL // LANES, tp, LANES), F32))
    in_specs = [
        pl.BlockSpec((1, tm, D_MODEL), lambda b, i: (b, i, 0)),
        pl.BlockSpec((1, 1, D_MODEL), lambda b, i: (b, 0, 0)),
        pl.BlockSpec((1, 1, D_MODEL), lambda b, i: (b, 0, 0)),
    ]
    w_all, layer = w
    in_specs += [pl.BlockSpec((None, D_MODEL, n), lambda b, i, blk=blk: (layer, 0, blk),
                              pipeline_mode=pl.Buffered(1)) for n, blk in blocks]
    out_specs = [pl.BlockSpec((1, dil, tm // dil, n), lambda b, i: (b, 0, i, 0))
                 for n, _ in blocks]
    out_shape = [jax.ShapeDtypeStruct((bsz, dil, s // dil, n), BF16) for n, _ in blocks]
    args = [x, shift, scale] + [w_all] * len(blocks)
    if dt_params is not None:
        assert dil == 1
        nc, cpb = s // SSM_CHUNK, tm // SSM_CHUNK
        row = pl.BlockSpec((1, LANES), lambda b, i: (0, 0))
        tok = pl.BlockSpec((1, tm, LANES), lambda b, i: (b, i, 0))
        tr = pl.BlockSpec((1, cpb, LANES, SSM_CHUNK), lambda b, i: (b, i, 0, 0))
        dt_blk = dt_params[0]
        in_specs += [pl.BlockSpec((None, D_MODEL, LANES), lambda b, i: (layer, 0, dt_blk)),
                     row, row]
        out_specs += [tok, tok, tr, tr]
        out_shape += [jax.ShapeDtypeStruct((bsz, s, LANES), F32)] * 2
        out_shape += [jax.ShapeDtypeStruct((bsz, nc, LANES, SSM_CHUNK), F32)] * 2
        args += [w_all, *dt_params[1:]]
    return pl.pallas_call(
        functools.partial(_inproj_kernel, dil=dil, tm=tm, tn=tn, n_w=len(blocks),
                          with_dt=dt_params is not None, lead_scale=lead_scale),
        grid=(bsz, s // tm),
        in_specs=in_specs,
        out_specs=out_specs,
        out_shape=out_shape,
        scratch_shapes=scratch,
        compiler_params=_params("parallel", "parallel"),
        name=f"inproj_d{dil}",
    )(*args)


def _attn_kernel(q_ref, kp_ref, k_ref, kn_ref, vp_ref, v_ref, vn_ref,
                 o_ref, lse_ref, kcat, vcat, bias, *, tq, length, dil):
    m = pl.program_id(2)
    r = A_RADIUS
    kcat[0:r] = kp_ref[0, 0]
    kcat[r:r + tq] = k_ref[0, 0]
    kcat[r + tq:r + tq + r] = kn_ref[0, 0]
    vcat[0:r] = vp_ref[0, 0]
    vcat[r:r + tq] = v_ref[0, 0]
    vcat[r + tq:r + tq + r] = vn_ref[0, 0]

    @pl.when((pl.program_id(0) == 0) & (pl.program_id(1) == 0) & (m == 0))
    def _():
        row = lax.broadcasted_iota(jnp.int32, (A_QB, A_SPAN), 0)
        col = lax.broadcasted_iota(jnp.int32, (A_QB, A_SPAN), 1)
        adist = jnp.abs(col - r - row)
        band = adist <= r
        for t, valid in enumerate((band, band & (col >= r), band & (col < A_SPAN - r))):
            dist = jnp.where(valid, adist.astype(F32) * float(dil), MASKED_DIST)
            for h in range(A_HEADS):
                bias[t * A_HEADS + h] = -(LOG2E * 2.0 ** (-8.0 * (h + 1) / A_HEADS)) * dist

    lane = lax.broadcasted_iota(jnp.int32, (A_QB, LANES), 1)
    lo_half = lane < A_HEAD_DIM
    ones = jnp.ones((A_SPAN, LANES), BF16)

    def sub_block(j, carry):
        r0 = pl.multiple_of(j * A_QB, A_QB)
        base = m * tq + j * A_QB - r
        variant = jnp.where(base < 0, 1, jnp.where(base + A_SPAN > length, 2, 0))
        m_tile = jnp.zeros((A_QB, LANES), F32)
        z_tile = jnp.ones((A_QB, LANES), F32)
        for hp in range(A_HEADS // 2):
            cs = slice(hp * LANES, (hp + 1) * LANES)
            q_pair = q_ref[0, 0, pl.ds(r0, A_QB), cs]
            k_pair = kcat[pl.ds(r0, A_SPAN), cs]
            v_ext = jnp.concatenate([vcat[pl.ds(r0, A_SPAN), cs], ones], axis=1)
            outs = []
            for a in range(2):
                h = 2 * hp + a
                keep = lo_half if a == 0 else jnp.logical_not(lo_half)
                qm = jnp.where(keep, q_pair, jnp.zeros_like(q_pair))
                sc = _dot(qm, k_pair, _NT) + bias[variant * A_HEADS + h]
                mx = jnp.max(sc, axis=-1, keepdims=True)
                p = jnp.exp2(sc - mx)
                oe = _dot(p.astype(BF16), v_ext)
                z = oe[:, LANES:]
                outs.append(oe[:, :LANES] / z)
                m_tile = jnp.where(lane == h, mx, m_tile)
                z_tile = jnp.where(lane == h, z, z_tile)
            o_pair = jnp.where(lo_half, outs[0], outs[1])
            o_ref[0, 0, pl.ds(r0, A_QB), cs] = o_pair.astype(o_ref.dtype)
        lse_ref[0, 0, pl.ds(r0, A_QB), :] = (m_tile + jnp.log2(z_tile)) * (1.0 / LOG2E)
        return carry

    lax.fori_loop(0, tq // A_QB, sub_block, 0, unroll=True)


def _dilated_attention(qkv, tq):
    bsz, dil, length, _ = qkv.shape
    assert length >= 2 * A_QB, "a score tile may touch only one end of the sequence"
    tq = min(tq, length)
    nblk = length // tq
    hb = tq // A_RADIUS
    nhalo = length // A_RADIUS

    main = lambda w: pl.BlockSpec((1, 1, tq, A_WIDTH), lambda b, rr, m: (b, rr, m, w))
    prev = lambda w: pl.BlockSpec(
        (1, 1, A_RADIUS, A_WIDTH), lambda b, rr, m: (b, rr, jnp.maximum(m * hb - 1, 0), w))
    nxt = lambda w: pl.BlockSpec(
        (1, 1, A_RADIUS, A_WIDTH),
        lambda b, rr, m: (b, rr, jnp.minimum((m + 1) * hb, nhalo - 1), w))

    return pl.pallas_call(
        functools.partial(_attn_kernel, tq=tq, length=length, dil=dil),
        grid=(bsz, dil, nblk),
        in_specs=[main(0), prev(1), main(1), nxt(1), prev(2), main(2), nxt(2)],
        out_specs=[
            pl.BlockSpec((1, 1, tq, A_WIDTH), lambda b, rr, m: (b, rr, m, 0)),
            pl.BlockSpec((1, 1, tq, LANES), lambda b, rr, m: (b, rr, m, 0)),
        ],
        out_shape=[
            jax.ShapeDtypeStruct((bsz, dil, length, A_WIDTH), BF16),
            jax.ShapeDtypeStruct((bsz, dil, length, LANES), F32),
        ],
        scratch_shapes=[
            pltpu.VMEM((tq + 2 * A_RADIUS, A_WIDTH), BF16),
            pltpu.VMEM((tq + 2 * A_RADIUS, A_WIDTH), BF16),
            pltpu.VMEM((3 * A_HEADS, A_QB, A_SPAN), F32),
        ],
        compiler_params=_params("arbitrary", "arbitrary", "arbitrary"),
        name=f"dilated_attn_d{dil}",
    )(qkv, qkv, qkv, qkv, qkv, qkv, qkv)


def _amerge_kernel(o1_ref, o2_ref, o3_ref, l1_ref, l2_ref, l3_ref, gate_ref, x_ref,
                   e_ref, w_ref, gm_ref, lg_ref, lb_ref, out_ref, o_nat, l_nat, *, tm):
    for slot, (o_ref, l_ref) in enumerate(((o2_ref, l2_ref), (o3_ref, l3_ref))):
        dil = o_ref.shape[1]
        n = tm // dil
        for r in range(dil):
            o_r = o_ref[0, r].astype(F32)
            for cb in range(A_WIDTH // LANES):
                o_nat[slot, cb, pl.ds(r, n, stride=dil), :] = o_r[:, cb * LANES:(cb + 1) * LANES]
            l_nat[slot, pl.ds(r, n, stride=dil), :] = l_ref[0, r]
    natural = lambda slot: jnp.concatenate(
        [o_nat[slot, cb] for cb in range(A_WIDTH // LANES)], axis=1)
    l1, l2, l3 = l1_ref[0, 0], l_nat[0], l_nat[1]
    lmax = jnp.maximum(jnp.maximum(l1, l2), l3)
    e1, e2, e3 = jnp.exp(l1 - lmax), jnp.exp(l2 - lmax), jnp.exp(l3 - lmax)
    inv = 1.0 / (e1 + e2 + e3)
    expand = e_ref[...]
    acc = None
    for e, o in ((e1, o1_ref[0, 0].astype(F32)), (e2, natural(0)), (e3, natural(1))):
        wexp = _dot(jnp.concatenate(_split2(e * inv), axis=1), expand)
        term = wexp * o
        acc = term if acc is None else acc + term
    y = acc * _silu(gate_ref[0, 0].astype(F32))
    out = _dot(y.astype(BF16), w_ref[...])
    res = DEEPNORM_ALPHA * x_ref[0] + gm_ref[0] * out
    out_ref[0] = _layer_norm_rows(res, lg_ref[...], lb_ref[...])


def _amerge_out(os_, ls_, gate, x, w_out, gate_mod, ln_g, ln_b, tm):
    bsz, s, _ = x.shape
    expand = np.zeros((2, LANES, A_WIDTH), np.float32)
    for h in range(A_HEADS):
        expand[:, h, h * A_HEAD_DIM:(h + 1) * A_HEAD_DIM] = 1.0
    expand = jnp.asarray(expand.reshape(2 * LANES, A_WIDTH), BF16)
    tok = lambda w: pl.BlockSpec((1, tm, w), lambda b, i: (b, i, 0))

    def grp(arr, blk=0):
        dil, w = arr.shape[1], (A_WIDTH if arr.shape[3] > LANES else LANES)
        return pl.BlockSpec((1, dil, tm // dil, w), lambda b, i: (b, 0, i, blk))

    return pl.pallas_call(
        functools.partial(_amerge_kernel, tm=tm),
        grid=(bsz, s // tm),
        in_specs=[
            grp(os_[0]), grp(os_[1]), grp(os_[2]), grp(ls_[0]), grp(ls_[1]), grp(ls_[2]),
            grp(gate),
            tok(D_MODEL),
            pl.BlockSpec((2 * LANES, A_WIDTH), lambda b, i: (0, 0)),
            pl.BlockSpec((None, A_WIDTH, D_MODEL), lambda b, i: (w_out[1], 0, 0)),
            pl.BlockSpec((1, 1, D_MODEL), lambda b, i: (b, 0, 0)),
            pl.BlockSpec((1, D_MODEL), lambda b, i: (0, 0)),
            pl.BlockSpec((1, D_MODEL), lambda b, i: (0, 0)),
        ],
        out_specs=tok(D_MODEL),
        out_shape=jax.ShapeDtypeStruct((bsz, s, D_MODEL), F32),
        scratch_shapes=[
            pltpu.VMEM((2, A_WIDTH // LANES, tm, LANES), F32),
            pltpu.VMEM((2, tm, LANES), F32),
        ],
        compiler_params=_params("parallel", "parallel"),
        name="attn_merge_out",
    )(*os_, *ls_, gate, x, expand, w_out[0], gate_mod, ln_g, ln_b)


def _conv_kernel(p_ref, m_ref, n_ref, s_ref, w_ref, b_ref, o_ref, cat, *, tm, nblk):
    i = pl.program_id(1)
    hal = BF16_ROWS
    cat[0:hal] = jnp.where(i > 0, p_ref[0], jnp.zeros_like(p_ref[0]))
    cat[hal:hal + tm] = m_ref[0]
    cat[hal + tm:hal + tm + hal] = jnp.where(i < nblk - 1, n_ref[0], jnp.zeros_like(n_ref[0]))
    half = SSM_CONV // 2
    taps = [k for k in range(SSM_CONV) if k != half]

    def row_block(blk, carry):
        r0 = pl.multiple_of(blk * SSM_CHUNK, SSM_CHUNK)
        for strip in range(o_ref.shape[2] // CONV_STRIP):
            cs = slice(strip * CONV_STRIP, (strip + 1) * CONV_STRIP)
            win = cat[pl.ds(r0, SSM_CHUNK + 2 * hal), cs]
            centre = cat[pl.ds(pl.multiple_of(r0 + hal, hal), SSM_CHUNK), cs]
            acc = centre.astype(F32) * w_ref[half:half + 1, cs]
            shifted = _dot(s_ref[...], win)
            for idx, k in enumerate(taps):
                acc = acc + shifted[idx * SSM_CHUNK:(idx + 1) * SSM_CHUNK] * w_ref[k:k + 1, cs]
            o_ref[0, pl.ds(r0, SSM_CHUNK), cs] = _silu(acc + b_ref[:, cs]).astype(o_ref.dtype)
        return carry

    lax.fori_loop(0, tm // SSM_CHUNK, row_block, 0, unroll=True)


def _conv_silu(zx, conv_w, conv_b, tm):
    bsz, s, _ = zx.shape
    tc = 1024
    first = SSM_INNER // tc
    nblk = s // tm
    hb = tm // BF16_ROWS
    nh = s // BF16_ROWS
    half = SSM_CONV // 2
    win = SSM_CHUNK + 2 * BF16_ROWS
    shifts = np.zeros((SSM_CONV - 1, SSM_CHUNK, win), np.float32)
    for idx, k in enumerate(k for k in range(SSM_CONV) if k != half):
        shifts[idx, np.arange(SSM_CHUNK), np.arange(SSM_CHUNK) + BF16_ROWS + k - half] = 1.0
    shifts = jnp.asarray(shifts.reshape(-1, win), BF16)
    return pl.pallas_call(
        functools.partial(_conv_kernel, tm=tm, nblk=nblk),
        grid=(bsz, nblk, SSM_CONV_DIM // tc),
        in_specs=[
            pl.BlockSpec((1, BF16_ROWS, tc),
                         lambda b, i, j: (b, jnp.maximum(i * hb - 1, 0), first + j)),
            pl.BlockSpec((1, tm, tc), lambda b, i, j: (b, i, first + j)),
            pl.BlockSpec((1, BF16_ROWS, tc),
                         lambda b, i, j: (b, jnp.minimum((i + 1) * hb, nh - 1), first + j)),
            pl.BlockSpec(((SSM_CONV - 1) * SSM_CHUNK, win), lambda b, i, j: (0, 0)),
            pl.BlockSpec((SSM_CONV, tc), lambda b, i, j: (0, j)),
            pl.BlockSpec((1, tc), lambda b, i, j: (0, j)),
        ],
        out_specs=pl.BlockSpec((1, tm, tc), lambda b, i, j: (b, i, j)),
        out_shape=jax.ShapeDtypeStruct((bsz, s, SSM_CONV_DIM), BF16),
        scratch_shapes=[pltpu.VMEM((tm + 2 * BF16_ROWS, tc), BF16)],
        compiler_params=_params("parallel", "parallel", "parallel"),
        name="ssd_conv",
    )(zx, zx, zx, shifts, conv_w, conv_b.reshape(1, SSM_CONV_DIM))


def _ssd_kernel(*refs, reverse, cps):
    if reverse:
        (xs_ref, b_ref, c_ref, dt_ref, cum_ref, dtt_ref, cumt_ref, e_ref,
         yf_ref, z_ref, d_ref, nw_ref, y_ref, state) = refs
    else:
        (xs_ref, b_ref, c_ref, dt_ref, cum_ref, dtt_ref, cumt_ref, e_ref,
         y_ref, state) = refs
    off = SSM_HEADS if reverse else 0
    far = 0 if reverse else SSM_CHUNK - 1
    L = SSM_CHUNK

    @pl.when(pl.program_id(1) == 0)
    def _():
        state[...] = jnp.zeros_like(state)

    lane = lax.broadcasted_iota(jnp.int32, (L, LANES), 1)
    own = (lane >= off) & (lane < off + SSM_HEADS)
    lo_half = lane < SSM_HEAD_DIM
    ri = lax.broadcasted_iota(jnp.int32, (L, L), 0)
    ci = lax.broadcasted_iota(jnp.int32, (L, L), 1)
    mask = (ci >= ri) if reverse else (ci <= ri)

    def chunk(step, carry):
        cc = (cps - 1 - step) if reverse else step
        rows = pl.ds(pl.multiple_of(cc * L, L), L)
        xs = xs_ref[0, rows, :]
        dt = dt_ref[0, rows, :]
        cum = cum_ref[0, rows, :]
        dtt = dtt_ref[0, cc]
        cumt = cumt_ref[0, cc]
        tot = cum[far:far + 1, :]
        w_state = dt * jnp.exp(jnp.where(own, tot - cum, 0.0))
        e_cum = jnp.exp(jnp.where(own, cum, 0.0))
        ws_hi, ws_lo = _split2(w_state)
        ec_hi, ec_lo = _split2(e_cum)
        lhs = jnp.concatenate([jnp.concatenate([ws_hi, ws_lo], axis=1),
                               jnp.concatenate([ec_hi, ec_lo], axis=1)], axis=0)
        both = _dot(lhs, e_ref[...])
        ws_exp, ec_exp = both[:L], both[L:]
        xs_f = xs.astype(F32)
        xw = (xs_f * ws_exp).astype(BF16)
        col_e = cum * LOG2E
        row_e = (cumt - jnp.log(dtt)) * LOG2E

        y_groups = []
        for g in range(SSM_GROUPS):
            ns = slice(g * SSM_STATE, (g + 1) * SSM_STATE)
            gs = slice(g * SSM_GROUP_COLS, (g + 1) * SSM_GROUP_COLS)
            bg = b_ref[0, rows, ns]
            cg = c_ref[0, rows, ns]
            cb = _dot(cg, bg, _NT)
            st = state[g]
            y_off = _dot(cg, st.astype(BF16)) * ec_exp[:, gs]
            diag = []
            for pp in range(SSM_HEADS_PER_GROUP // 2):
                h0 = g * SSM_HEADS_PER_GROUP + 2 * pp
                lms = []
                for a in range(2):
                    col = off + h0 + a
                    diff = col_e[:, col:col + 1] - row_e[col:col + 1, :]
                    lmat = jnp.where(mask, jnp.exp2(diff) * cb, 0.0)
                    lms.append(lmat.astype(BF16))
                xp = xs[:, h0 * SSM_HEAD_DIM:(h0 + 2) * SSM_HEAD_DIM]
                zero = jnp.zeros_like(xp)
                rhs = jnp.concatenate([jnp.where(lo_half, xp, zero),
                                       jnp.where(lo_half, zero, xp)], axis=0)
                diag.append(_dot(jnp.concatenate(lms, axis=1), rhs))
            y_groups.append(jnp.concatenate(diag, axis=1) + y_off)
            state[g] = ec_exp[far:far + 1, gs] * st + _dot(bg, xw[:, gs], _TN)
        y = jnp.concatenate(y_groups, axis=1)

        if reverse:
            y = y + yf_ref[0, rows, :].astype(F32) + d_ref[...] * xs_f
            y = y * _silu(z_ref[0, rows, :].astype(F32))
            ms = jnp.mean(y * y, axis=-1, keepdims=True)
            y = y * lax.rsqrt(ms + RMS_EPS) * nw_ref[...]
        y_ref[0, rows, :] = y.astype(y_ref.dtype)
        return carry

    lax.fori_loop(0, cps, chunk, 0, unroll=True)


def _ssd_scan(xbc, dt, cum, dtt, cumt, reverse, cps, extra=None):
    bsz, s, _ = xbc.shape
    nblk = s // (cps * SSM_CHUNK)
    off = SSM_HEADS if reverse else 0
    expand = np.zeros((2, LANES, SSM_INNER), np.float32)
    for h in range(SSM_HEADS):
        expand[:, off + h, h * SSM_HEAD_DIM:(h + 1) * SSM_HEAD_DIM] = 1.0
    expand = jnp.asarray(expand.reshape(2 * LANES, SSM_INNER), BF16)
    cidx = (lambda c: nblk - 1 - c) if reverse else (lambda c: c)
    b_blk = SSM_INNER // (SSM_GROUPS * SSM_STATE)
    tok = lambda w, blk=0: pl.BlockSpec((1, cps * SSM_CHUNK, w), lambda b, c: (b, cidx(c), blk))
    tr = pl.BlockSpec((1, cps, LANES, SSM_CHUNK), lambda b, c: (b, cidx(c), 0, 0))
    in_specs = [
        tok(SSM_INNER), tok(SSM_GROUPS * SSM_STATE, b_blk), tok(SSM_GROUPS * SSM_STATE, b_blk + 1),
        tok(LANES), tok(LANES), tr, tr,
        pl.BlockSpec((2 * LANES, SSM_INNER), lambda b, c: (0, 0)),
    ]
    args = [xbc, xbc, xbc, dt, cum, dtt, cumt, expand]
    if reverse:
        yf, zx, d_exp, norm_w = extra
        row = pl.BlockSpec((1, SSM_INNER), lambda b, c: (0, 0))
        in_specs += [tok(SSM_INNER), tok(SSM_INNER), row, row]
        args += [yf, zx, d_exp, norm_w]
    return pl.pallas_call(
        functools.partial(_ssd_kernel, reverse=reverse, cps=cps),
        grid=(bsz, nblk),
        in_specs=in_specs,
        out_specs=tok(SSM_INNER),
        out_shape=jax.ShapeDtypeStruct((bsz, s, SSM_INNER), BF16),
        scratch_shapes=[pltpu.VMEM((SSM_GROUPS, SSM_STATE, SSM_GROUP_COLS), F32)],
        compiler_params=_params("parallel", "arbitrary"),
        name="ssd_bwd" if reverse else "ssd_fwd",
    )(*args)


def _out_kernel(y_ref, x_ref, w_ref, gm_ref, lg_ref, lb_ref, out_ref, *, parts):
    rows_per = y_ref.shape[1] // parts
    for part in range(parts):
        rows = slice(part * rows_per, (part + 1) * rows_per)
        out = _dot(y_ref[0, rows, :], w_ref[...])
        res = DEEPNORM_ALPHA * x_ref[0, rows, :] + gm_ref[0] * out
        out_ref[0, rows, :] = _layer_norm_rows(res, lg_ref[...], lb_ref[...])


def _out_proj(y, x, w_out, gate_mod, ln_g, ln_b, tm):
    bsz, s, k = y.shape
    return pl.pallas_call(
        functools.partial(_out_kernel, parts=tm // 512),
        grid=(bsz, s // tm),
        in_specs=[
            pl.BlockSpec((1, tm, k), lambda b, i: (b, i, 0)),
            pl.BlockSpec((1, tm, D_MODEL), lambda b, i: (b, i, 0)),
            pl.BlockSpec((None, k, D_MODEL), lambda b, i: (w_out[1], 0, 0)),
            pl.BlockSpec((1, 1, D_MODEL), lambda b, i: (b, 0, 0)),
            pl.BlockSpec((1, D_MODEL), lambda b, i: (0, 0)),
            pl.BlockSpec((1, D_MODEL), lambda b, i: (0, 0)),
        ],
        out_specs=pl.BlockSpec((1, tm, D_MODEL), lambda b, i: (b, i, 0)),
        out_shape=jax.ShapeDtypeStruct((bsz, s, D_MODEL), F32),
        compiler_params=_params("parallel", "parallel"),
        name="out_proj_ln",
    )(y, x, w_out[0], gate_mod, ln_g, ln_b)


def _layer_a(x, shift, scale, gate_mod, w_in, w_out, ln_g, ln_b):
    gw = 3 * A_WIDTH
    q_scale = LOG2E / math.sqrt(A_HEAD_DIM)
    os_, ls_ = [], []
    for g, (_, dil) in enumerate(DILATION_PAIRS):
        blocks = [(gw, g)]
        if g == 0:
            blocks.append((A_WIDTH, N_DIL * gw // A_WIDTH))
        qkv, *rest = _inproj(x, shift, scale, w_in, blocks, tm=1024, tn=A_WIDTH, dil=dil,
                             lead_scale=q_scale)
        if g == 0:
            gate = rest[0]
        o, lse = _dilated_attention(qkv, tq=1024)
        os_.append(o)
        ls_.append(lse)
    return _amerge_out(os_, ls_, gate, x, w_out, gate_mod, ln_g, ln_b, tm=512)


def _layer_b(x, shift, scale, gate_mod, w_in, conv_w, conv_b, dt_bias, a_log, d_skip,
             norm_w, w_out, ln_g, ln_b):
    n_main = SSM_INNER + SSM_CONV_DIM
    pad = lambda v: jnp.zeros((1, LANES), F32).at[0, :2 * SSM_HEADS].set(v.reshape(-1))
    zx, dt, cum, dtt, cumt = _inproj(x, shift, scale, w_in, [(n_main, 0)], tm=512, tn=1024,
                                     dt_params=(n_main // LANES, pad(dt_bias), pad(a_log)))
    zx = zx[:, 0]
    xbc = _conv_silu(zx, conv_w, conv_b, tm=2048)
    yf = _ssd_scan(xbc, dt, cum, dtt, cumt, reverse=False, cps=4)
    d_exp = jnp.repeat(d_skip.astype(F32), SSM_HEAD_DIM).reshape(1, SSM_INNER)
    yn = _ssd_scan(xbc, dt, cum, dtt, cumt, reverse=True, cps=4,
                   extra=(yf, zx, d_exp, norm_w.reshape(1, SSM_INNER).astype(F32)))
    return _out_proj(yn, x, w_out, gate_mod, ln_g, ln_b, tm=1024)


def kernel(x, c, ada_w, ada_b, ln_g, ln_b, a_w_in, a_w_out, b_w_in, b_conv_w, b_conv_b,
           b_dt_bias, b_a_log, b_d, b_norm_w, b_w_out):
    bsz = x.shape[0]
    mod = _modulation(c, ada_w, ada_b)
    a_w_in, a_w_out, b_w_out = (w.astype(BF16) for w in (a_w_in, a_w_out, b_w_out))
    lane_pad = -b_w_in.shape[2] % LANES
    b_w_in = jnp.pad(b_w_in, ((0, 0), (0, 0), (0, lane_pad))).astype(BF16)
    for i in range(DEPTH):
        m3 = mod[i].reshape(bsz, 3, 1, D_MODEL)
        shift, scale, gate_mod = m3[:, 0], m3[:, 1], m3[:, 2]
        lg = ln_g[i].reshape(1, D_MODEL)
        lb = ln_b[i].reshape(1, D_MODEL)
        j = i // 2
        if i % 2 == 0:
            x = _layer_a(x, shift, scale, gate_mod, (a_w_in, j), (a_w_out, j), lg, lb)
        else:
            x = _layer_b(x, shift, scale, gate_mod, (b_w_in, j), b_conv_w[j], b_conv_b[j],
                         b_dt_bias[j], b_a_log[j], b_d[j], b_norm_w[j], (b_w_out, j), lg, lb)
    return x
```

```python
import functools
import math

import jax
import jax.numpy as jnp
import numpy as np
from jax import lax
from jax.experimental import pallas as pl
from jax.experimental.pallas import tpu as pltpu

D_MODEL = 1024
DEPTH = 4

A_HEADS = 16
A_HEAD_DIM = 64
A_WIDTH = A_HEADS * A_HEAD_DIM
DILATION_PAIRS = ((128, 1), (512, 4), (2048, 16))
N_DIL = len(DILATION_PAIRS)
A_IN_COLS = N_DIL * 3 * A_WIDTH + A_WIDTH
A_RADIUS = 64
A_QB = 128
A_SPAN = A_QB + 2 * A_RADIUS

SSM_INNER = 2 * D_MODEL
SSM_HEAD_DIM = 64
SSM_HEADS = SSM_INNER // SSM_HEAD_DIM
SSM_STATE = 128
SSM_GROUPS = 4
SSM_CONV = 5
SSM_CHUNK = 128
SSM_CONV_DIM = SSM_INNER + 2 * SSM_GROUPS * SSM_STATE
SSM_GROUP_COLS = SSM_INNER // SSM_GROUPS
SSM_HEADS_PER_GROUP = SSM_HEADS // SSM_GROUPS

DEEPNORM_ALPHA = (2 * DEPTH) ** 0.25
LN_EPS = 1e-5
RMS_EPS = 1e-5

LANES = 128
BF16_ROWS = 16
MASKED_DIST = 1e30
LOG2E = math.log2(math.e)
CONV_STRIP = 512
VMEM_LIMIT = 56 * 1024 * 1024

F32 = jnp.float32
BF16 = jnp.bfloat16

_NT = (((1,), (1,)), ((), ()))
_TN = (((0,), (0,)), ((), ()))


def _params(*sem):
    return pltpu.CompilerParams(dimension_semantics=sem, vmem_limit_bytes=VMEM_LIMIT)


def _dot(a, b, dims=None):
    if dims is None:
        return jnp.dot(a, b, preferred_element_type=F32)
    return lax.dot_general(a, b, dims, preferred_element_type=F32)


def _split2(a):
    hi = a.astype(BF16)
    lo = (a - hi.astype(F32)).astype(BF16)
    return hi, lo


def _split3(a):
    hi = a.astype(BF16)
    r = a - hi.astype(F32)
    mid = r.astype(BF16)
    lo = (r - mid.astype(F32)).astype(BF16)
    return hi, mid, lo


def _silu(x):
    return x / (1.0 + jnp.exp2(x * (-LOG2E)))


def _layer_norm_rows(r, g, b):
    mu = jnp.mean(r, axis=-1, keepdims=True)
    d = r - mu
    var = jnp.mean(d * d, axis=-1, keepdims=True)
    return d * lax.rsqrt(var + LN_EPS) * g + b


def _mod_kernel(c_ref, w_ref, b_ref, o_ref):
    cond = _silu(c_ref[...])
    w = w_ref[0]
    c_hi, c_lo = _split2(cond)
    w_hi, w_lo = _split2(w)
    acc = _dot(c_hi, w_hi) + _dot(c_lo, w_hi) + _dot(c_hi, w_lo)
    o_ref[0] = acc + b_ref[0]


def _modulation(c, ada_w, ada_b):
    bsz = c.shape[0]
    rows = 8
    cp = jnp.zeros((rows, D_MODEL), F32).at[:bsz].set(c)
    tn = 1024
    out = pl.pallas_call(
        _mod_kernel,
        grid=(DEPTH, 3 * D_MODEL // tn),
        in_specs=[
            pl.BlockSpec((rows, D_MODEL), lambda i, j: (0, 0)),
            pl.BlockSpec((1, D_MODEL, tn), lambda i, j: (i, 0, j)),
            pl.BlockSpec((1, 1, tn), lambda i, j: (i, 0, j)),
        ],
        out_specs=pl.BlockSpec((1, rows, tn), lambda i, j: (i, 0, j)),
        out_shape=jax.ShapeDtypeStruct((DEPTH, rows, 3 * D_MODEL), F32),
        compiler_params=_params("parallel", "parallel"),
        name="ada_mod",
    )(cp, ada_w, ada_b.reshape(DEPTH, 1, 3 * D_MODEL))
    return out[:, :bsz]


def _dt_tail(hb, w_ref, bias_ref, alog_ref, dt_ref, cum_ref, dtt_ref, cumt_ref, tm):
    raw = _dot(hb, w_ref[...]) + bias_ref[...]
    dt = jnp.maximum(raw, 0.0) + jnp.log(1.0 + jnp.exp(-jnp.abs(raw)))
    a = dt * (-jnp.exp(alog_ref[...]))
    dt_ref[0] = dt
    ri = lax.broadcasted_iota(jnp.int32, (SSM_CHUNK, SSM_CHUNK), 0)
    ci = lax.broadcasted_iota(jnp.int32, (SSM_CHUNK, SSM_CHUNK), 1)
    tri = jnp.concatenate([(ci <= ri).astype(BF16), (ci >= ri).astype(BF16)], axis=0)
    lane = lax.broadcasted_iota(jnp.int32, (SSM_CHUNK, LANES), 1)
    fwd_cols = lane < SSM_HEADS
    for c in range(tm // SSM_CHUNK):
        rows = slice(c * SSM_CHUNK, (c + 1) * SSM_CHUNK)
        both = _dot(tri, jnp.concatenate(_split3(a[rows]), axis=1))
        both = both[:, :LANES] + both[:, LANES:2 * LANES] + both[:, 2 * LANES:]
        cum = jnp.where(fwd_cols, both[:SSM_CHUNK], both[SSM_CHUNK:])
        cum_ref[0, rows, :] = cum
        dtt_ref[0, c] = dt[rows].T
        cumt_ref[0, c] = cum.T


def _inproj_kernel(*refs, dil, tm, tn, n_w, with_dt, lead_scale):
    x_ref, sh_ref, sc_ref = refs[:3]
    w_refs, refs = refs[3:3 + n_w], refs[3 + n_w:]
    if with_dt:
        dt_in, refs = refs[:3], refs[3:]
    o_refs, refs = refs[:n_w], refs[n_w:]
    if with_dt:
        dt_out, refs = refs[:4], refs[4:]
    h_ref, *hf_ref = refs
    parts = h_ref.shape[0]
    tp = tm // parts
    n = tp // dil
    for part in range(parts):
        h = x_ref[0, part * tp:(part + 1) * tp, :] * (1.0 + sc_ref[0]) + sh_ref[0]
        if dil == 1:
            h_ref[part] = h.astype(BF16)
        else:
            for cb in range(D_MODEL // LANES):
                cs = slice(cb * LANES, (cb + 1) * LANES)
                hf_ref[0][part, cb] = h[:, cs]
                for r in range(dil):
                    h_ref[part, r * n:(r + 1) * n, cs] = (
                        hf_ref[0][part, cb, pl.ds(r, n, stride=dil), :].astype(BF16))

        hb = h_ref[part]
        for k, (w_ref, o_ref) in enumerate(zip(w_refs, o_refs)):
            for j in range(w_ref.shape[1] // tn):
                cols = slice(j * tn, (j + 1) * tn)
                res = _dot(hb, w_ref[:, cols])
                if lead_scale is not None and k == 0 and j == 0:
                    res = res * lead_scale
                for r in range(dil):
                    o_ref[0, r, part * n:(part + 1) * n, cols] = (
                        res[r * n:(r + 1) * n].astype(o_ref.dtype))
    if with_dt:
        assert parts == 1
        _dt_tail(hb, *dt_in, *dt_out, tm)


def _inproj(x, shift, scale, w, blocks, tm, tn, dil=1, dt_params=None, lead_scale=None):
    bsz, s, _ = x.shape
    parts = tm // 512
    tp = tm // parts
    scratch = [pltpu.VMEM((parts, tp, D_MODEL), BF16)]
    if dil > 1:
        scratch.append(pltpu.VMEM((parts, D_MODEL // LANES, tp, LANES), F32))
    in_specs = [
        pl.BlockSpec((1, tm, D_MODEL), lambda b, i: (b, i, 0)),
        pl.BlockSpec((1, 1, D_MODEL), lambda b, i: (b, 0, 0)),
        pl.BlockSpec((1, 1, D_MODEL), lambda b, i: (b, 0, 0)),
    ]
    w_all, layer = w
    in_specs += [pl.BlockSpec((None, D_MODEL, n), lambda b, i, blk=blk: (layer, 0, blk),
                              pipeline_mode=pl.Buffered(1)) for n, blk in blocks]
    out_specs = [pl.BlockSpec((1, dil, tm // dil, n), lambda b, i: (b, 0, i, 0))
                 for n, _ in blocks]
    out_shape = [jax.ShapeDtypeStruct((bsz, dil, s // dil, n), BF16) for n, _ in blocks]
    args = [x, shift, scale] + [w_all] * len(blocks)
    if dt_params is not None:
        assert dil == 1
        nc, cpb = s // SSM_CHUNK, tm // SSM_CHUNK
        row = pl.BlockSpec((1, LANES), lambda b, i: (0, 0))
        tok = pl.BlockSpec((1, tm, LANES), lambda b, i: (b, i, 0))
        tr = pl.BlockSpec((1, cpb, LANES, SSM_CHUNK), lambda b, i: (b, i, 0, 0))
        dt_blk = dt_params[0]
        in_specs += [pl.BlockSpec((None, D_MODEL, LANES), lambda b, i: (layer, 0, dt_blk)),
                     row, row]
        out_specs += [tok, tok, tr, tr]
        out_shape += [jax.ShapeDtypeStruct((bsz, s, LANES), F32)] * 2
        out_shape += [jax.ShapeDtypeStruct((bsz, nc, LANES, SSM_CHUNK), F32)] * 2
        args += [w_all, *dt_params[1:]]
    return pl.pallas_call(
        functools.partial(_inproj_kernel, dil=dil, tm=tm, tn=tn, n_w=len(blocks),
                          with_dt=dt_params is not None, lead_scale=lead_scale),
        grid=(bsz, s // tm),
        in_specs=in_specs,
        out_specs=out_specs,
        out_shape=out_shape,
        scratch_shapes=scratch,
        compiler_params=_params("parallel", "parallel"),
        name=f"inproj_d{dil}",
    )(*args)


def _attn_kernel(q_ref, kp_ref, k_ref, kn_ref, vp_ref, v_ref, vn_ref,
                 o_ref, lse_ref, kcat, vcat, bias, *, tq, length, dil):
    m = pl.program_id(2)
    r = A_RADIUS
    kcat[0:r] = kp_ref[0, 0]
    kcat[r:r + tq] = k_ref[0, 0]
    kcat[r + tq:r + tq + r] = kn_ref[0, 0]
    vcat[0:r] = vp_ref[0, 0]
    vcat[r:r + tq] = v_ref[0, 0]
    vcat[r + tq:r + tq + r] = vn_ref[0, 0]

    @pl.when((pl.program_id(0) == 0) & (pl.program_id(1) == 0) & (m == 0))
    def _():
        row = lax.broadcasted_iota(jnp.int32, (A_QB, A_SPAN), 0)
        col = lax.broadcasted_iota(jnp.int32, (A_QB, A_SPAN), 1)
        adist = jnp.abs(col - r - row)
        band = adist <= r
        for t, valid in enumerate((band, band & (col >= r), band & (col < A_SPAN - r))):
            dist = jnp.where(valid, adist.astype(F32) * float(dil), MASKED_DIST)
            for h in range(A_HEADS):
                bias[t * A_HEADS + h] = -(LOG2E * 2.0 ** (-8.0 * (h + 1) / A_HEADS)) * dist

    lane = lax.broadcasted_iota(jnp.int32, (A_QB, LANES), 1)
    lo_half = lane < A_HEAD_DIM
    ones = jnp.ones((A_SPAN, LANES), BF16)

    def sub_block(j, carry):
        r0 = pl.multiple_of(j * A_QB, A_QB)
        base = m * tq + j * A_QB - r
        variant = jnp.where(base < 0, 1, jnp.where(base + A_SPAN > length, 2, 0))
        m_tile = jnp.zeros((A_QB, LANES), F32)
        z_tile = jnp.ones((A_QB, LANES), F32)
        for hp in range(A_HEADS // 2):
            cs = slice(hp * LANES, (hp + 1) * LANES)
            q_pair = q_ref[0, 0, pl.ds(r0, A_QB), cs]
            k_pair = kcat[pl.ds(r0, A_SPAN), cs]
            v_ext = jnp.concatenate([vcat[pl.ds(r0, A_SPAN), cs], ones], axis=1)
            outs = []
            for a in range(2):
                h = 2 * hp + a
                keep = lo_half if a == 0 else jnp.logical_not(lo_half)
                qm = jnp.where(keep, q_pair, jnp.zeros_like(q_pair))
                sc = _dot(qm, k_pair, _NT) + bias[variant * A_HEADS + h]
                mx = jnp.max(sc, axis=-1, keepdims=True)
                p = jnp.exp2(sc - mx)
                oe = _dot(p.astype(BF16), v_ext)
                z = oe[:, LANES:]
                outs.append(oe[:, :LANES] / z)
                m_tile = jnp.where(lane == h, mx, m_tile)
                z_tile = jnp.where(lane == h, z, z_tile)
            o_pair = jnp.where(lo_half, outs[0], outs[1])
            o_ref[0, 0, pl.ds(r0, A_QB), cs] = o_pair.astype(o_ref.dtype)
        lse_ref[0, 0, pl.ds(r0, A_QB), :] = (m_tile + jnp.log2(z_tile)) * (1.0 / LOG2E)
        return carry

    lax.fori_loop(0, tq // A_QB, sub_block, 0, unroll=True)


def _dilated_attention(qkv, tq):
    bsz, dil, length, _ = qkv.shape
    assert length >= 2 * A_QB, "a score tile may touch only one end of the sequence"
    tq = min(tq, length)
    nblk = length // tq
    hb = tq // A_RADIUS
    nhalo = length // A_RADIUS

    main = lambda w: pl.BlockSpec((1, 1, tq, A_WIDTH), lambda b, rr, m: (b, rr, m, w))
    prev = lambda w: pl.BlockSpec(
        (1, 1, A_RADIUS, A_WIDTH), lambda b, rr, m: (b, rr, jnp.maximum(m * hb - 1, 0), w))
    nxt = lambda w: pl.BlockSpec(
        (1, 1, A_RADIUS, A_WIDTH),
        lambda b, rr, m: (b, rr, jnp.minimum((m + 1) * hb, nhalo - 1), w))

    return pl.pallas_call(
        functools.partial(_attn_kernel, tq=tq, length=length, dil=dil),
        grid=(bsz, dil, nblk),
        in_specs=[main(0), prev(1), main(1), nxt(1), prev(2), main(2), nxt(2)],
        out_specs=[
            pl.BlockSpec((1, 1, tq, A_WIDTH), lambda b, rr, m: (b, rr, m, 0)),
            pl.BlockSpec((1, 1, tq, LANES), lambda b, rr, m: (b, rr, m, 0)),
        ],
        out_shape=[
            jax.ShapeDtypeStruct((bsz, dil, length, A_WIDTH), BF16),
            jax.ShapeDtypeStruct((bsz, dil, length, LANES), F32),
        ],
        scratch_shapes=[
            pltpu.VMEM((tq + 2 * A_RADIUS, A_WIDTH), BF16),
            pltpu.VMEM((tq + 2 * A_RADIUS, A_WIDTH), BF16),
            pltpu.VMEM((3 * A_HEADS, A_QB, A_SPAN), F32),
        ],
        compiler_params=_params("arbitrary", "arbitrary", "arbitrary"),
        name=f"dilated_attn_d{dil}",
    )(qkv, qkv, qkv, qkv, qkv, qkv, qkv)


def _amerge_kernel(o1_ref, o2_ref, o3_ref, l1_ref, l2_ref, l3_ref, gate_ref, x_ref,
                   e_ref, w_ref, gm_ref, lg_ref, lb_ref, out_ref, o_nat, l_nat, *, tm):
    for slot, (o_ref, l_ref) in enumerate(((o2_ref, l2_ref), (o3_ref, l3_ref))):
        dil = o_ref.shape[1]
        n = tm // dil
        for r in range(dil):
            o_r = o_ref[0, r].astype(F32)
            for cb in range(A_WIDTH // LANES):
                o_nat[slot, cb, pl.ds(r, n, stride=dil), :] = o_r[:, cb * LANES:(cb + 1) * LANES]
            l_nat[slot, pl.ds(r, n, stride=dil), :] = l_ref[0, r]
    natural = lambda slot: jnp.concatenate(
        [o_nat[slot, cb] for cb in range(A_WIDTH // LANES)], axis=1)
    l1, l2, l3 = l1_ref[0, 0], l_nat[0], l_nat[1]
    lmax = jnp.maximum(jnp.maximum(l1, l2), l3)
    e1, e2, e3 = jnp.exp(l1 - lmax), jnp.exp(l2 - lmax), jnp.exp(l3 - lmax)
    inv = 1.0 / (e1 + e2 + e3)
    expand = e_ref[...]
    acc = None
    for e, o in ((e1, o1_ref[0, 0].astype(F32)), (e2, natural(0)), (e3, natural(1))):
        wexp = _dot(jnp.concatenate(_split2(e * inv), axis=1), expand)
        term = wexp * o
        acc = term if acc is None else acc + term
    y = acc * _silu(gate_ref[0, 0].astype(F32))
    out = _dot(y.astype(BF16), w_ref[...])
    res = DEEPNORM_ALPHA * x_ref[0] + gm_ref[0] * out
    out_ref[0] = _layer_norm_rows(res, lg_ref[...], lb_ref[...])


def _amerge_out(os_, ls_, gate, x, w_out, gate_mod, ln_g, ln_b, tm):
    bsz, s, _ = x.shape
    expand = np.zeros((2, LANES, A_WIDTH), np.float32)
    for h in range(A_HEADS):
        expand[:, h, h * A_HEAD_DIM:(h + 1) * A_HEAD_DIM] = 1.0
    expand = jnp.asarray(expand.reshape(2 * LANES, A_WIDTH), BF16)
    tok = lambda w: pl.BlockSpec((1, tm, w), lambda b, i: (b, i, 0))

    def grp(arr, blk=0):
        dil, w = arr.shape[1], (A_WIDTH if arr.shape[3] > LANES else LANES)
        return pl.BlockSpec((1, dil, tm // dil, w), lambda b, i: (b, 0, i, blk))

    return pl.pallas_call(
        functools.partial(_amerge_kernel, tm=tm),
        grid=(bsz, s // tm),
        in_specs=[
            grp(os_[0]), grp(os_[1]), grp(os_[2]), grp(ls_[0]), grp(ls_[1]), grp(ls_[2]),
            grp(gate),
            tok(D_MODEL),
            pl.BlockSpec((2 * LANES, A_WIDTH), lambda b, i: (0, 0)),
            pl.BlockSpec((None, A_WIDTH, D_MODEL), lambda b, i: (w_out[1], 0, 0)),
            pl.BlockSpec((1, 1, D_MODEL), lambda b, i: (b, 0, 0)),
            pl.BlockSpec((1, D_MODEL), lambda b, i: (0, 0)),
            pl.BlockSpec((1, D_MODEL), lambda b, i: (0, 0)),
        ],
        out_specs=tok(D_MODEL),
        out_shape=jax.ShapeDtypeStruct((bsz, s, D_MODEL), F32),
        scratch_shapes=[
            pltpu.VMEM((2, A_WIDTH // LANES, tm, LANES), F32),
            pltpu.VMEM((2, tm, LANES), F32),
        ],
        compiler_params=_params("parallel", "parallel"),
        name="attn_merge_out",
    )(*os_, *ls_, gate, x, expand, w_out[0], gate_mod, ln_g, ln_b)


def _conv_kernel(p_ref, m_ref, n_ref, s_ref, w_ref, b_ref, o_ref, cat, *, tm, nblk):
    i = pl.program_id(1)
    hal = BF16_ROWS
    cat[0:hal] = jnp.where(i > 0, p_ref[0], jnp.zeros_like(p_ref[0]))
    cat[hal:hal + tm] = m_ref[0]
    cat[hal + tm:hal + tm + hal] = jnp.where(i < nblk - 1, n_ref[0], jnp.zeros_like(n_ref[0]))
    half = SSM_CONV // 2
    taps = [k for k in range(SSM_CONV) if k != half]

    def row_block(blk, carry):
        r0 = pl.multiple_of(blk * SSM_CHUNK, SSM_CHUNK)
        for strip in range(o_ref.shape[2] // CONV_STRIP):
            cs = slice(strip * CONV_STRIP, (strip + 1) * CONV_STRIP)
            win = cat[pl.ds(r0, SSM_CHUNK + 2 * hal), cs]
            centre = cat[pl.ds(pl.multiple_of(r0 + hal, hal), SSM_CHUNK), cs]
            acc = centre.astype(F32) * w_ref[half:half + 1, cs]
            shifted = _dot(s_ref[...], win)
            for idx, k in enumerate(taps):
                acc = acc + shifted[idx * SSM_CHUNK:(idx + 1) * SSM_CHUNK] * w_ref[k:k + 1, cs]
            o_ref[0, pl.ds(r0, SSM_CHUNK), cs] = _silu(acc + b_ref[:, cs]).astype(o_ref.dtype)
        return carry

    lax.fori_loop(0, tm // SSM_CHUNK, row_block, 0, unroll=True)


def _conv_silu(zx, conv_w, conv_b, tm):
    bsz, s, _ = zx.shape
    tc = 1024
    first = SSM_INNER // tc
    nblk = s // tm
    hb = tm // BF16_ROWS
    nh = s // BF16_ROWS
    half = SSM_CONV // 2
    win = SSM_CHUNK + 2 * BF16_ROWS
    shifts = np.zeros((SSM_CONV - 1, SSM_CHUNK, win), np.float32)
    for idx, k in enumerate(k for k in range(SSM_CONV) if k != half):
        shifts[idx, np.arange(SSM_CHUNK), np.arange(SSM_CHUNK) + BF16_ROWS + k - half] = 1.0
    shifts = jnp.asarray(shifts.reshape(-1, win), BF16)
    return pl.pallas_call(
        functools.partial(_conv_kernel, tm=tm, nblk=nblk),
        grid=(bsz, nblk, SSM_CONV_DIM // tc),
        in_specs=[
            pl.BlockSpec((1, BF16_ROWS, tc),
                         lambda b, i, j: (b, jnp.maximum(i * hb - 1, 0), first + j)),
            pl.BlockSpec((1, tm, tc), lambda b, i, j: (b, i, first + j)),
            pl.BlockSpec((1, BF16_ROWS, tc),
                         lambda b, i, j: (b, jnp.minimum((i + 1) * hb, nh - 1), first + j)),
            pl.BlockSpec(((SSM_CONV - 1) * SSM_CHUNK, win), lambda b, i, j: (0, 0)),
            pl.BlockSpec((SSM_CONV, tc), lambda b, i, j: (0, j)),
            pl.BlockSpec((1, tc), lambda b, i, j: (0, j)),
        ],
        out_specs=pl.BlockSpec((1, tm, tc), lambda b, i, j: (b, i, j)),
        out_shape=jax.ShapeDtypeStruct((bsz, s, SSM_CONV_DIM), BF16),
        scratch_shapes=[pltpu.VMEM((tm + 2 * BF16_ROWS, tc), BF16)],
        compiler_params=_params("parallel", "parallel", "parallel"),
        name="ssd_conv",
    )(zx, zx, zx, shifts, conv_w, conv_b.reshape(1, SSM_CONV_DIM))


def _ssd_kernel(*refs, reverse, cps):
    if reverse:
        (xs_ref, b_ref, c_ref, dt_ref, cum_ref, dtt_ref, cumt_ref, e_ref,
         yf_ref, z_ref, d_ref, nw_ref, y_ref, state) = refs
    else:
        (xs_ref, b_ref, c_ref, dt_ref, cum_ref, dtt_ref, cumt_ref, e_ref,
         y_ref, state) = refs
    off = SSM_HEADS if reverse else 0
    far = 0 if reverse else SSM_CHUNK - 1
    L = SSM_CHUNK

    @pl.when(pl.program_id(1) == 0)
    def _():
        state[...] = jnp.zeros_like(state)

    lane = lax.broadcasted_iota(jnp.int32, (L, LANES), 1)
    own = (lane >= off) & (lane < off + SSM_HEADS)
    lo_half = lane < SSM_HEAD_DIM
    ri = lax.broadcasted_iota(jnp.int32, (L, L), 0)
    ci = lax.broadcasted_iota(jnp.int32, (L, L), 1)
    mask = (ci >= ri) if reverse else (ci <= ri)

    def chunk(step, carry):
        cc = (cps - 1 - step) if reverse else step
        rows = pl.ds(pl.multiple_of(cc * L, L), L)
        xs = xs_ref[0, rows, :]
        dt = dt_ref[0, rows, :]
        cum = cum_ref[0, rows, :]
        dtt = dtt_ref[0, cc]
        cumt = cumt_ref[0, cc]
        tot = cum[far:far + 1, :]
        w_state = dt * jnp.exp(jnp.where(own, tot - cum, 0.0))
        e_cum = jnp.exp(jnp.where(own, cum, 0.0))
        ws_hi, ws_lo = _split2(w_state)
        ec_hi, ec_lo = _split2(e_cum)
        lhs = jnp.concatenate([jnp.concatenate([ws_hi, ws_lo], axis=1),
                               jnp.concatenate([ec_hi, ec_lo], axis=1)], axis=0)
        both = _dot(lhs, e_ref[...])
        ws_exp, ec_exp = both[:L], both[L:]
        xs_f = xs.astype(F32)
        xw = (xs_f * ws_exp).astype(BF16)
        col_e = cum * LOG2E
        row_e = (cumt - jnp.log(dtt)) * LOG2E

        y_groups = []
        for g in range(SSM_GROUPS):
            ns = slice(g * SSM_STATE, (g + 1) * SSM_STATE)
            gs = slice(g * SSM_GROUP_COLS, (g + 1) * SSM_GROUP_COLS)
            bg = b_ref[0, rows, ns]
            cg = c_ref[0, rows, ns]
            cb = _dot(cg, bg, _NT)
            st = state[g]
            y_off = _dot(cg, st.astype(BF16)) * ec_exp[:, gs]
            diag = []
            for pp in range(SSM_HEADS_PER_GROUP // 2):
                h0 = g * SSM_HEADS_PER_GROUP + 2 * pp
                lms = []
                for a in range(2):
                    col = off + h0 + a
                    diff = col_e[:, col:col + 1] - row_e[col:col + 1, :]
                    lmat = jnp.where(mask, jnp.exp2(diff) * cb, 0.0)
                    lms.append(lmat.astype(BF16))
                xp = xs[:, h0 * SSM_HEAD_DIM:(h0 + 2) * SSM_HEAD_DIM]
                zero = jnp.zeros_like(xp)
                rhs = jnp.concatenate([jnp.where(lo_half, xp, zero),
                                       jnp.where(lo_half, zero, xp)], axis=0)
                diag.append(_dot(jnp.concatenate(lms, axis=1), rhs))
            y_groups.append(jnp.concatenate(diag, axis=1) + y_off)
            state[g] = ec_exp[far:far + 1, gs] * st + _dot(bg, xw[:, gs], _TN)
        y = jnp.concatenate(y_groups, axis=1)

        if reverse:
            y = y + yf_ref[0, rows, :].astype(F32) + d_ref[...] * xs_f
            y = y * _silu(z_ref[0, rows, :].astype(F32))
            ms = jnp.mean(y * y, axis=-1, keepdims=True)
            y = y * lax.rsqrt(ms + RMS_EPS) * nw_ref[...]
        y_ref[0, rows, :] = y.astype(y_ref.dtype)
        return carry

    lax.fori_loop(0, cps, chunk, 0, unroll=True)


def _ssd_scan(xbc, dt, cum, dtt, cumt, reverse, cps, extra=None):
    bsz, s, _ = xbc.shape
    nblk = s // (cps * SSM_CHUNK)
    off = SSM_HEADS if reverse else 0
    expand = np.zeros((2, LANES, SSM_INNER), np.float32)
    for h in range(SSM_HEADS):
        expand[:, off + h, h * SSM_HEAD_DIM:(h + 1) * SSM_HEAD_DIM] = 1.0
    expand = jnp.asarray(expand.reshape(2 * LANES, SSM_INNER), BF16)
    cidx = (lambda c: nblk - 1 - c) if reverse else (lambda c: c)
    b_blk = SSM_INNER // (SSM_GROUPS * SSM_STATE)
    tok = lambda w, blk=0: pl.BlockSpec((1, cps * SSM_CHUNK, w), lambda b, c: (b, cidx(c), blk))
    tr = pl.BlockSpec((1, cps, LANES, SSM_CHUNK), lambda b, c: (b, cidx(c), 0, 0))
    in_specs = [
        tok(SSM_INNER), tok(SSM_GROUPS * SSM_STATE, b_blk), tok(SSM_GROUPS * SSM_STATE, b_blk + 1),
        tok(LANES), tok(LANES), tr, tr,
        pl.BlockSpec((2 * LANES, SSM_INNER), lambda b, c: (0, 0)),
    ]
    args = [xbc, xbc, xbc, dt, cum, dtt, cumt, expand]
    if reverse:
        yf, zx, d_exp, norm_w = extra
        row = pl.BlockSpec((1, SSM_INNER), lambda b, c: (0, 0))
        in_specs += [tok(SSM_INNER), tok(SSM_INNER), row, row]
        args += [yf, zx, d_exp, norm_w]
    return pl.pallas_call(
        functools.partial(_ssd_kernel, reverse=reverse, cps=cps),
        grid=(bsz, nblk),
        in_specs=in_specs,
        out_specs=tok(SSM_INNER),
        out_shape=jax.ShapeDtypeStruct((bsz, s, SSM_INNER), BF16),
        scratch_shapes=[pltpu.VMEM((SSM_GROUPS, SSM_STATE, SSM_GROUP_COLS), F32)],
        compiler_params=_params("parallel", "arbitrary"),
        name="ssd_bwd" if reverse else "ssd_fwd",
    )(*args)


def _out_kernel(y_ref, x_ref, w_ref, gm_ref, lg_ref, lb_ref, out_ref, *, parts):
    rows_per = y_ref.shape[1] // parts
    for part in range(parts):
        rows = slice(part * rows_per, (part + 1) * rows_per)
        out = _dot(y_ref[0, rows, :], w_ref[...])
        res = DEEPNORM_ALPHA * x_ref[0, rows, :] + gm_ref[0] * out
        out_ref[0, rows, :] = _layer_norm_rows(res, lg_ref[...], lb_ref[...])


def _out_proj(y, x, w_out, gate_mod, ln_g, ln_b, tm):
    bsz, s, k = y.shape
    return pl.pallas_call(
        functools.partial(_out_kernel, parts=tm // 512),
        grid=(bsz, s // tm),
        in_specs=[
            pl.BlockSpec((1, tm, k), lambda b, i: (b, i, 0)),
            pl.BlockSpec((1, tm, D_MODEL), lambda b, i: (b, i, 0)),
            pl.BlockSpec((None, k, D_MODEL), lambda b, i: (w_out[1], 0, 0)),
            pl.BlockSpec((1, 1, D_MODEL), lambda b, i: (b, 0, 0)),
            pl.BlockSpec((1, D_MODEL), lambda b, i: (0, 0)),
            pl.BlockSpec((1, D_MODEL), lambda b, i: (0, 0)),
        ],
        out_specs=pl.BlockSpec((1, tm, D_MODEL), lambda b, i: (b, i, 0)),
        out_shape=jax.ShapeDtypeStruct((bsz, s, D_MODEL), F32),
        compiler_params=_params("parallel", "parallel"),
        name="out_proj_ln",
    )(y, x, w_out[0], gate_mod, ln_g, ln_b)


def _layer_a(x, shift, scale, gate_mod, w_in, w_out, ln_g, ln_b):
    gw = 3 * A_WIDTH
    q_scale = LOG2E / math.sqrt(A_HEAD_DIM)
    os_, ls_ = [], []
    for g, (_, dil) in enumerate(DILATION_PAIRS):
        blocks = [(gw, g)]
        if g == 0:
            blocks.append((A_WIDTH, N_DIL * gw // A_WIDTH))
        qkv, *rest = _inproj(x, shift, scale, w_in, blocks, tm=1024, tn=A_WIDTH, dil=dil,
                             lead_scale=q_scale)
        if g == 0:
            gate = rest[0]
        o, lse = _dilated_attention(qkv, tq=1024)
        os_.append(o)
        ls_.append(lse)
    return _amerge_out(os_, ls_, gate, x, w_out, gate_mod, ln_g, ln_b, tm=512)


def _layer_b(x, shift, scale, gate_mod, w_in, conv_w, conv_b, dt_bias, a_log, d_skip,
             norm_w, w_out, ln_g, ln_b):
    n_main = SSM_INNER + SSM_CONV_DIM
    pad = lambda v: jnp.zeros((1, LANES), F32).at[0, :2 * SSM_HEADS].set(v.reshape(-1))
    zx, dt, cum, dtt, cumt = _inproj(x, shift, scale, w_in, [(n_main, 0)], tm=512, tn=1024,
                                     dt_params=(n_main // LANES, pad(dt_bias), pad(a_log)))
    zx = zx[:, 0]
    xbc = _conv_silu(zx, conv_w, conv_b, tm=2048)
    yf = _ssd_scan(xbc, dt, cum, dtt, cumt, reverse=False, cps=8)
    d_exp = jnp.repeat(d_skip.astype(F32), SSM_HEAD_DIM).reshape(1, SSM_INNER)
    yn = _ssd_scan(xbc, dt, cum, dtt, cumt, reverse=True, cps=8,
                   extra=(yf, zx, d_exp, norm_w.reshape(1, SSM_INNER).astype(F32)))
    return _out_proj(yn, x, w_out, gate_mod, ln_g, ln_b, tm=1024)


def kernel(x, c, ada_w, ada_b, ln_g, ln_b, a_w_in, a_w_out, b_w_in, b_conv_w, b_conv_b,
           b_dt_bias, b_a_log, b_d, b_norm_w, b_w_out):
    bsz = x.shape[0]
    mod = _modulation(c, ada_w, ada_b)
    a_w_in, a_w_out, b_w_out = (w.astype(BF16) for w in (a_w_in, a_w_out, b_w_out))
    lane_pad = -b_w_in.shape[2] % LANES
    b_w_in = jnp.pad(b_w_in, ((0, 0), (0, 0), (0, lane_pad))).astype(BF16)
    for i in range(DEPTH):
        m3 = mod[i].reshape(bsz, 3, 1, D_MODEL)
        shift, scale, gate_mod = m3[:, 0], m3[:, 1], m3[:, 2]
        lg = ln_g[i].reshape(1, D_MODEL)
        lb = ln_b[i].reshape(1, D_MODEL)
        j = i // 2
        if i % 2 == 0:
            x = _layer_a(x, shift, scale, gate_mod, (a_w_in, j), (a_w_out, j), lg, lb)
        else:
            x = _layer_b(x, shift, scale, gate_mod, (b_w_in, j), b_conv_w[j], b_conv_b[j],
                         b_dt_bias[j], b_a_log[j], b_d[j], b_norm_w[j], (b_w_out, j), lg, lb)
    return x
```

```python
import functools
import math

import jax
import jax.numpy as jnp
import numpy as np
from jax import lax
from jax.experimental import pallas as pl
from jax.experimental.pallas import tpu as pltpu

D_MODEL = 1024
DEPTH = 4

A_HEADS = 16
A_HEAD_DIM = 64
A_WIDTH = A_HEADS * A_HEAD_DIM
DILATION_PAIRS = ((128, 1), (512, 4), (2048, 16))
N_DIL = len(DILATION_PAIRS)
A_IN_COLS = N_DIL * 3 * A_WIDTH + A_WIDTH
A_RADIUS = 64
A_QB = 128
A_SPAN = A_QB + 2 * A_RADIUS

SSM_INNER = 2 * D_MODEL
SSM_HEAD_DIM = 64
SSM_HEADS = SSM_INNER // SSM_HEAD_DIM
SSM_STATE = 128
SSM_GROUPS = 4
SSM_CONV = 5
SSM_CHUNK = 128
SSM_CONV_DIM = SSM_INNER + 2 * SSM_GROUPS * SSM_STATE
SSM_GROUP_COLS = SSM_INNER // SSM_GROUPS
SSM_HEADS_PER_GROUP = SSM_HEADS // SSM_GROUPS

DEEPNORM_ALPHA = (2 * DEPTH) ** 0.25
LN_EPS = 1e-5
RMS_EPS = 1e-5

LANES = 128
BF16_ROWS = 16
MASKED_DIST = 1e30
LOG2E = math.log2(math.e)
CONV_STRIP = 512
VMEM_LIMIT = 48 * 1024 * 1024

F32 = jnp.float32
BF16 = jnp.bfloat16

_NT = (((1,), (1,)), ((), ()))
_TN = (((0,), (0,)), ((), ()))


def _params(*sem, fuse_inputs=None):
    return pltpu.CompilerParams(dimension_semantics=sem, vmem_limit_bytes=VMEM_LIMIT,
                                allow_input_fusion=fuse_inputs)


def _dot(a, b, dims=None):
    if dims is None:
        return jnp.dot(a, b, preferred_element_type=F32)
    return lax.dot_general(a, b, dims, preferred_element_type=F32)


def _split2(a):
    hi = a.astype(BF16)
    lo = (a - hi.astype(F32)).astype(BF16)
    return hi, lo


def _split3(a):
    hi = a.astype(BF16)
    r = a - hi.astype(F32)
    mid = r.astype(BF16)
    lo = (r - mid.astype(F32)).astype(BF16)
    return hi, mid, lo


def _silu(x):
    return x / (1.0 + jnp.exp2(x * (-LOG2E)))


def _layer_norm_rows(r, g, b):
    mu = jnp.mean(r, axis=-1, keepdims=True)
    d = r - mu
    var = jnp.mean(d * d, axis=-1, keepdims=True)
    return d * lax.rsqrt(var + LN_EPS) * g + b


def _mod_kernel(c_ref, w_ref, b_ref, o_ref):
    cond = _silu(c_ref[...])
    w = w_ref[0]
    c_hi, c_lo = _split2(cond)
    w_hi, w_lo = _split2(w)
    acc = _dot(c_hi, w_hi) + _dot(c_lo, w_hi) + _dot(c_hi, w_lo)
    o_ref[0] = acc + b_ref[0]


def _modulation(c, ada_w, ada_b):
    bsz = c.shape[0]
    rows = 8
    cp = jnp.zeros((rows, D_MODEL), F32).at[:bsz].set(c)
    tn = 1024
    out = pl.pallas_call(
        _mod_kernel,
        grid=(DEPTH, 3 * D_MODEL // tn),
        in_specs=[
            pl.BlockSpec((rows, D_MODEL), lambda i, j: (0, 0)),
            pl.BlockSpec((1, D_MODEL, tn), lambda i, j: (i, 0, j)),
            pl.BlockSpec((1, 1, tn), lambda i, j: (i, 0, j)),
        ],
        out_specs=pl.BlockSpec((1, rows, tn), lambda i, j: (i, 0, j)),
        out_shape=jax.ShapeDtypeStruct((DEPTH, rows, 3 * D_MODEL), F32),
        compiler_params=_params("parallel", "parallel"),
        name="ada_mod",
    )(cp, ada_w, ada_b.reshape(DEPTH, 1, 3 * D_MODEL))
    return out[:, :bsz]


def _dt_tail(hb, w_ref, bias_ref, alog_ref, dt_ref, cum_ref, dtt_ref, cumt_ref, tm):
    raw = _dot(hb, w_ref[...]) + bias_ref[...]
    dt = jnp.maximum(raw, 0.0) + jnp.log(1.0 + jnp.exp(-jnp.abs(raw)))
    a = dt * (-jnp.exp(alog_ref[...]))
    dt_ref[0] = dt
    ri = lax.broadcasted_iota(jnp.int32, (SSM_CHUNK, SSM_CHUNK), 0)
    ci = lax.broadcasted_iota(jnp.int32, (SSM_CHUNK, SSM_CHUNK), 1)
    tri = jnp.concatenate([(ci <= ri).astype(BF16), (ci >= ri).astype(BF16)], axis=0)
    lane = lax.broadcasted_iota(jnp.int32, (SSM_CHUNK, LANES), 1)
    fwd_cols = lane < SSM_HEADS
    for c in range(tm // SSM_CHUNK):
        rows = slice(c * SSM_CHUNK, (c + 1) * SSM_CHUNK)
        both = _dot(tri, jnp.concatenate(_split3(a[rows]), axis=1))
        both = both[:, :LANES] + both[:, LANES:2 * LANES] + both[:, 2 * LANES:]
        cum = jnp.where(fwd_cols, both[:SSM_CHUNK], both[SSM_CHUNK:])
        cum_ref[0, rows, :] = cum
        dtt_ref[0, c] = dt[rows].T
        cumt_ref[0, c] = cum.T


def _inproj_kernel(*refs, dil, tm, tn, n_w, with_dt, lead_scale):
    x_ref, sh_ref, sc_ref = refs[:3]
    w_refs, refs = refs[3:3 + n_w], refs[3 + n_w:]
    if with_dt:
        dt_in, refs = refs[:3], refs[3:]
    o_refs, refs = refs[:n_w], refs[n_w:]
    if with_dt:
        dt_out, refs = refs[:4], refs[4:]
    h_ref, *hf_ref = refs
    parts = h_ref.shape[0]
    tp = tm // parts
    n = tp // dil
    for part in range(parts):
        h = x_ref[0, part * tp:(part + 1) * tp, :] * (1.0 + sc_ref[0]) + sh_ref[0]
        if dil == 1:
            h_ref[part] = h.astype(BF16)
        else:
            for cb in range(D_MODEL // LANES):
                cs = slice(cb * LANES, (cb + 1) * LANES)
                hf_ref[0][part, cb] = h[:, cs]
                for r in range(dil):
                    h_ref[part, r * n:(r + 1) * n, cs] = (
                        hf_ref[0][part, cb, pl.ds(r, n, stride=dil), :].astype(BF16))

        hb = h_ref[part]
        for k, (w_ref, o_ref) in enumerate(zip(w_refs, o_refs)):
            for j in range(w_ref.shape[1] // tn):
                cols = slice(j * tn, (j + 1) * tn)
                res = _dot(hb, w_ref[:, cols])
                if lead_scale is not None and k == 0 and j == 0:
                    res = res * lead_scale
                for r in range(dil):
                    o_ref[0, r, part * n:(part + 1) * n, cols] = (
                        res[r * n:(r + 1) * n].astype(o_ref.dtype))
    if with_dt:
        assert parts == 1
        _dt_tail(hb, *dt_in, *dt_out, tm)


def _inproj(x, shift, scale, w, blocks, tm, tn, dil=1, dt_params=None, lead_scale=None):
    bsz, s, _ = x.shape
    parts = tm // 512
    tp = tm // parts
    scratch = [pltpu.VMEM((parts, tp, D_MODEL), BF16)]
    if dil > 1:
        scratch.append(pltpu.VMEM((parts, D_MODEL // LANES, tp, LANES), F32))
    in_specs = [
        pl.BlockSpec((1, tm, D_MODEL), lambda b, i: (b, i, 0)),
        pl.BlockSpec((1, 1, D_MODEL), lambda b, i: (b, 0, 0)),
        pl.BlockSpec((1, 1, D_MODEL), lambda b, i: (b, 0, 0)),
    ]
    w_all, layer = w
    in_specs += [pl.BlockSpec((None, D_MODEL, n), lambda b, i, blk=blk: (layer, 0, blk),
                              pipeline_mode=pl.Buffered(1)) for n, blk in blocks]
    out_specs = [pl.BlockSpec((1, dil, tm // dil, n), lambda b, i: (b, 0, i, 0))
                 for n, _ in blocks]
    out_shape = [jax.ShapeDtypeStruct((bsz, dil, s // dil, n), BF16) for n, _ in blocks]
    args = [x, shift, scale] + [w_all] * len(blocks)
    if dt_params is not None:
        assert dil == 1
        nc, cpb = s // SSM_CHUNK, tm // SSM_CHUNK
        row = pl.BlockSpec((1, LANES), lambda b, i: (0, 0))
        tok = pl.BlockSpec((1, tm, LANES), lambda b, i: (b, i, 0))
        tr = pl.BlockSpec((1, cpb, LANES, SSM_CHUNK), lambda b, i: (b, i, 0, 0))
        dt_blk = dt_params[0]
        in_specs += [pl.BlockSpec((None, D_MODEL, LANES), lambda b, i: (layer, 0, dt_blk)),
                     row, row]
        out_specs += [tok, tok, tr, tr]
        out_shape += [jax.ShapeDtypeStruct((bsz, s, LANES), F32)] * 2
        out_shape += [jax.ShapeDtypeStruct((bsz, nc, LANES, SSM_CHUNK), F32)] * 2
        args += [w_all, *dt_params[1:]]
    return pl.pallas_call(
        functools.partial(_inproj_kernel, dil=dil, tm=tm, tn=tn, n_w=len(blocks),
                          with_dt=dt_params is not None, lead_scale=lead_scale),
        grid=(bsz, s // tm),
        in_specs=in_specs,
        out_specs=out_specs,
        out_shape=out_shape,
        scratch_shapes=scratch,
        compiler_params=_params("parallel", "parallel",
                                fuse_inputs=[a is w_all for a in args]),
        name=f"inproj_d{dil}",
    )(*args)


def _attn_kernel(q_ref, kp_ref, k_ref, kn_ref, vp_ref, v_ref, vn_ref,
                 o_ref, lse_ref, kcat, vcat, bias, *, tq, length, dil):
    m = pl.program_id(2)
    r = A_RADIUS
    kcat[0:r] = kp_ref[0, 0]
    kcat[r:r + tq] = k_ref[0, 0]
    kcat[r + tq:r + tq + r] = kn_ref[0, 0]
    vcat[0:r] = vp_ref[0, 0]
    vcat[r:r + tq] = v_ref[0, 0]
    vcat[r + tq:r + tq + r] = vn_ref[0, 0]

    @pl.when((pl.program_id(0) == 0) & (pl.program_id(1) == 0) & (m == 0))
    def _():
        row = lax.broadcasted_iota(jnp.int32, (A_QB, A_SPAN), 0)
        col = lax.broadcasted_iota(jnp.int32, (A_QB, A_SPAN), 1)
        adist = jnp.abs(col - r - row)
        band = adist <= r
        for t, valid in enumerate((band, band & (col >= r), band & (col < A_SPAN - r))):
            dist = jnp.where(valid, adist.astype(F32) * float(dil), MASKED_DIST)
            for h in range(A_HEADS):
                bias[t * A_HEADS + h] = -(LOG2E * 2.0 ** (-8.0 * (h + 1) / A_HEADS)) * dist

    lane = lax.broadcasted_iota(jnp.int32, (A_QB, LANES), 1)
    lo_half = lane < A_HEAD_DIM
    ones = jnp.ones((A_SPAN, LANES), BF16)

    def sub_block(j, carry):
        r0 = pl.multiple_of(j * A_QB, A_QB)
        base = m * tq + j * A_QB - r
        variant = jnp.where(base < 0, 1, jnp.where(base + A_SPAN > length, 2, 0))
        m_tile = jnp.zeros((A_QB, LANES), F32)
        z_tile = jnp.ones((A_QB, LANES), F32)
        for hp in range(A_HEADS // 2):
            cs = slice(hp * LANES, (hp + 1) * LANES)
            q_pair = q_ref[0, 0, pl.ds(r0, A_QB), cs]
            k_pair = kcat[pl.ds(r0, A_SPAN), cs]
            v_ext = jnp.concatenate([vcat[pl.ds(r0, A_SPAN), cs], ones], axis=1)
            outs = []
            for a in range(2):
                h = 2 * hp + a
                keep = lo_half if a == 0 else jnp.logical_not(lo_half)
                qm = jnp.where(keep, q_pair, jnp.zeros_like(q_pair))
                sc = _dot(qm, k_pair, _NT) + bias[variant * A_HEADS + h]
                mx = jnp.max(sc, axis=-1, keepdims=True)
                p = jnp.exp2(sc - mx)
                oe = _dot(p.astype(BF16), v_ext)
                z = oe[:, LANES:]
                outs.append(oe[:, :LANES] / z)
                m_tile = jnp.where(lane == h, mx, m_tile)
                z_tile = jnp.where(lane == h, z, z_tile)
            o_pair = jnp.where(lo_half, outs[0], outs[1])
            o_ref[0, 0, pl.ds(r0, A_QB), cs] = o_pair.astype(o_ref.dtype)
        lse_ref[0, 0, pl.ds(r0, A_QB), :] = (m_tile + jnp.log2(z_tile)) * (1.0 / LOG2E)
        return carry

    lax.fori_loop(0, tq // A_QB, sub_block, 0, unroll=True)


def _dilated_attention(qkv, tq):
    bsz, dil, length, _ = qkv.shape
    assert length >= 2 * A_QB, "a score tile may touch only one end of the sequence"
    tq = min(tq, length)
    nblk = length // tq
    hb = tq // A_RADIUS
    nhalo = length // A_RADIUS

    main = lambda w: pl.BlockSpec((1, 1, tq, A_WIDTH), lambda b, rr, m: (b, rr, m, w))
    prev = lambda w: pl.BlockSpec(
        (1, 1, A_RADIUS, A_WIDTH), lambda b, rr, m: (b, rr, jnp.maximum(m * hb - 1, 0), w))
    nxt = lambda w: pl.BlockSpec(
        (1, 1, A_RADIUS, A_WIDTH),
        lambda b, rr, m: (b, rr, jnp.minimum((m + 1) * hb, nhalo - 1), w))

    return pl.pallas_call(
        functools.partial(_attn_kernel, tq=tq, length=length, dil=dil),
        grid=(bsz, dil, nblk),
        in_specs=[main(0), prev(1), main(1), nxt(1), prev(2), main(2), nxt(2)],
        out_specs=[
            pl.BlockSpec((1, 1, tq, A_WIDTH), lambda b, rr, m: (b, rr, m, 0)),
            pl.BlockSpec((1, 1, tq, LANES), lambda b, rr, m: (b, rr, m, 0)),
        ],
        out_shape=[
            jax.ShapeDtypeStruct((bsz, dil, length, A_WIDTH), BF16),
            jax.ShapeDtypeStruct((bsz, dil, length, LANES), F32),
        ],
        scratch_shapes=[
            pltpu.VMEM((tq + 2 * A_RADIUS, A_WIDTH), BF16),
            pltpu.VMEM((tq + 2 * A_RADIUS, A_WIDTH), BF16),
            pltpu.VMEM((3 * A_HEADS, A_QB, A_SPAN), F32),
        ],
        compiler_params=_params("arbitrary", "arbitrary", "arbitrary"),
        name=f"dilated_attn_d{dil}",
    )(qkv, qkv, qkv, qkv, qkv, qkv, qkv)


def _amerge_kernel(o1_ref, o2_ref, o3_ref, l1_ref, l2_ref, l3_ref, gate_ref, x_ref,
                   e_ref, w_ref, gm_ref, lg_ref, lb_ref, out_ref, o_nat, l_nat, *, tm):
    for slot, (o_ref, l_ref) in enumerate(((o2_ref, l2_ref), (o3_ref, l3_ref))):
        dil = o_ref.shape[1]
        n = tm // dil
        for r in range(dil):
            o_r = o_ref[0, r].astype(F32)
            for cb in range(A_WIDTH // LANES):
                o_nat[slot, cb, pl.ds(r, n, stride=dil), :] = o_r[:, cb * LANES:(cb + 1) * LANES]
            l_nat[slot, pl.ds(r, n, stride=dil), :] = l_ref[0, r]
    natural = lambda slot: jnp.concatenate(
        [o_nat[slot, cb] for cb in range(A_WIDTH // LANES)], axis=1)
    l1, l2, l3 = l1_ref[0, 0], l_nat[0], l_nat[1]
    lmax = jnp.maximum(jnp.maximum(l1, l2), l3)
    e1, e2, e3 = jnp.exp(l1 - lmax), jnp.exp(l2 - lmax), jnp.exp(l3 - lmax)
    inv = 1.0 / (e1 + e2 + e3)
    expand = e_ref[...]
    acc = None
    for e, o in ((e1, o1_ref[0, 0].astype(F32)), (e2, natural(0)), (e3, natural(1))):
        wexp = _dot(jnp.concatenate(_split2(e * inv), axis=1), expand)
        term = wexp * o
        acc = term if acc is None else acc + term
    y = acc * _silu(gate_ref[0, 0].astype(F32))
    out = _dot(y.astype(BF16), w_ref[...])
    res = DEEPNORM_ALPHA * x_ref[0] + gm_ref[0] * out
    out_ref[0] = _layer_norm_rows(res, lg_ref[...], lb_ref[...])


def _amerge_out(os_, ls_, gate, x, w_out, gate_mod, ln_g, ln_b, tm):
    bsz, s, _ = x.shape
    expand = np.zeros((2, LANES, A_WIDTH), np.float32)
    for h in range(A_HEADS):
        expand[:, h, h * A_HEAD_DIM:(h + 1) * A_HEAD_DIM] = 1.0
    expand = jnp.asarray(expand.reshape(2 * LANES, A_WIDTH), BF16)
    tok = lambda w: pl.BlockSpec((1, tm, w), lambda b, i: (b, i, 0))

    def grp(arr, blk=0):
        dil, w = arr.shape[1], (A_WIDTH if arr.shape[3] > LANES else LANES)
        return pl.BlockSpec((1, dil, tm // dil, w), lambda b, i: (b, 0, i, blk))

    return pl.pallas_call(
        functools.partial(_amerge_kernel, tm=tm),
        grid=(bsz, s // tm),
        in_specs=[
            grp(os_[0]), grp(os_[1]), grp(os_[2]), grp(ls_[0]), grp(ls_[1]), grp(ls_[2]),
            grp(gate),
            tok(D_MODEL),
            pl.BlockSpec((2 * LANES, A_WIDTH), lambda b, i: (0, 0)),
            pl.BlockSpec((None, A_WIDTH, D_MODEL), lambda b, i: (w_out[1], 0, 0)),
            pl.BlockSpec((1, 1, D_MODEL), lambda b, i: (b, 0, 0)),
            pl.BlockSpec((1, D_MODEL), lambda b, i: (0, 0)),
            pl.BlockSpec((1, D_MODEL), lambda b, i: (0, 0)),
        ],
        out_specs=tok(D_MODEL),
        out_shape=jax.ShapeDtypeStruct((bsz, s, D_MODEL), F32),
        scratch_shapes=[
            pltpu.VMEM((2, A_WIDTH // LANES, tm, LANES), F32),
            pltpu.VMEM((2, tm, LANES), F32),
        ],
        compiler_params=_params("parallel", "parallel"),
        name="attn_merge_out",
    )(*os_, *ls_, gate, x, expand, w_out[0], gate_mod, ln_g, ln_b)


def _conv_kernel(p_ref, m_ref, n_ref, s_ref, w_ref, b_ref, o_ref, cat, *, tm, nblk):
    i = pl.program_id(1)
    hal = BF16_ROWS
    cat[0:hal] = jnp.where(i > 0, p_ref[0], jnp.zeros_like(p_ref[0]))
    cat[hal:hal + tm] = m_ref[0]
    cat[hal + tm:hal + tm + hal] = jnp.where(i < nblk - 1, n_ref[0], jnp.zeros_like(n_ref[0]))
    half = SSM_CONV // 2
    taps = [k for k in range(SSM_CONV) if k != half]

    def row_block(blk, carry):
        r0 = pl.multiple_of(blk * SSM_CHUNK, SSM_CHUNK)
        for strip in range(o_ref.shape[2] // CONV_STRIP):
            cs = slice(strip * CONV_STRIP, (strip + 1) * CONV_STRIP)
            win = cat[pl.ds(r0, SSM_CHUNK + 2 * hal), cs]
            centre = cat[pl.ds(pl.multiple_of(r0 + hal, hal), SSM_CHUNK), cs]
            acc = centre.astype(F32) * w_ref[half:half + 1, cs]
            shifted = _dot(s_ref[...], win)
            for idx, k in enumerate(taps):
                acc = acc + shifted[idx * SSM_CHUNK:(idx + 1) * SSM_CHUNK] * w_ref[k:k + 1, cs]
            o_ref[0, pl.ds(r0, SSM_CHUNK), cs] = _silu(acc + b_ref[:, cs]).astype(o_ref.dtype)
        return carry

    lax.fori_loop(0, tm // SSM_CHUNK, row_block, 0, unroll=True)


def _conv_silu(zx, conv_w, conv_b, tm):
    bsz, s, _ = zx.shape
    tc = 1024
    first = SSM_INNER // tc
    nblk = s // tm
    hb = tm // BF16_ROWS
    nh = s // BF16_ROWS
    half = SSM_CONV // 2
    win = SSM_CHUNK + 2 * BF16_ROWS
    shifts = np.zeros((SSM_CONV - 1, SSM_CHUNK, win), np.float32)
    for idx, k in enumerate(k for k in range(SSM_CONV) if k != half):
        shifts[idx, np.arange(SSM_CHUNK), np.arange(SSM_CHUNK) + BF16_ROWS + k - half] = 1.0
    shifts = jnp.asarray(shifts.reshape(-1, win), BF16)
    return pl.pallas_call(
        functools.partial(_conv_kernel, tm=tm, nblk=nblk),
        grid=(bsz, nblk, SSM_CONV_DIM // tc),
        in_specs=[
            pl.BlockSpec((1, BF16_ROWS, tc),
                         lambda b, i, j: (b, jnp.maximum(i * hb - 1, 0), first + j)),
            pl.BlockSpec((1, tm, tc), lambda b, i, j: (b, i, first + j)),
            pl.BlockSpec((1, BF16_ROWS, tc),
                         lambda b, i, j: (b, jnp.minimum((i + 1) * hb, nh - 1), first + j)),
            pl.BlockSpec(((SSM_CONV - 1) * SSM_CHUNK, win), lambda b, i, j: (0, 0)),
            pl.BlockSpec((SSM_CONV, tc), lambda b, i, j: (0, j)),
            pl.BlockSpec((1, tc), lambda b, i, j: (0, j)),
        ],
        out_specs=pl.BlockSpec((1, tm, tc), lambda b, i, j: (b, i, j)),
        out_shape=jax.ShapeDtypeStruct((bsz, s, SSM_CONV_DIM), BF16),
        scratch_shapes=[pltpu.VMEM((tm + 2 * BF16_ROWS, tc), BF16)],
        compiler_params=_params("parallel", "parallel", "parallel"),
        name="ssd_conv",
    )(zx, zx, zx, shifts, conv_w, conv_b.reshape(1, SSM_CONV_DIM))


def _ssd_kernel(*refs, reverse, cps):
    if reverse:
        (xs_ref, b_ref, c_ref, dt_ref, cum_ref, dtt_ref, cumt_ref, e_ref,
         yf_ref, z_ref, d_ref, nw_ref, y_ref, state) = refs
    else:
        (xs_ref, b_ref, c_ref, dt_ref, cum_ref, dtt_ref, cumt_ref, e_ref,
         y_ref, state) = refs
    off = SSM_HEADS if reverse else 0
    far = 0 if reverse else SSM_CHUNK - 1
    L = SSM_CHUNK

    @pl.when(pl.program_id(1) == 0)
    def _():
        state[...] = jnp.zeros_like(state)

    lane = lax.broadcasted_iota(jnp.int32, (L, LANES), 1)
    own = (lane >= off) & (lane < off + SSM_HEADS)
    lo_half = lane < SSM_HEAD_DIM
    ri = lax.broadcasted_iota(jnp.int32, (L, L), 0)
    ci = lax.broadcasted_iota(jnp.int32, (L, L), 1)
    mask = (ci >= ri) if reverse else (ci <= ri)

    def chunk(step, carry):
        cc = (cps - 1 - step) if reverse else step
        rows = pl.ds(pl.multiple_of(cc * L, L), L)
        xs = xs_ref[0, rows, :]
        dt = dt_ref[0, rows, :]
        cum = cum_ref[0, rows, :]
        dtt = dtt_ref[0, cc]
        cumt = cumt_ref[0, cc]
        tot = cum[far:far + 1, :]
        w_state = dt * jnp.exp(jnp.where(own, tot - cum, 0.0))
        e_cum = jnp.exp(jnp.where(own, cum, 0.0))
        ws_hi, ws_lo = _split2(w_state)
        ec_hi, ec_lo = _split2(e_cum)
        lhs = jnp.concatenate([jnp.concatenate([ws_hi, ws_lo], axis=1),
                               jnp.concatenate([ec_hi, ec_lo], axis=1)], axis=0)
        both = _dot(lhs, e_ref[...])
        ws_exp, ec_exp = both[:L], both[L:]
        xs_f = xs.astype(F32)
        xw = (xs_f * ws_exp).astype(BF16)
        col_e = cum * LOG2E
        row_e = (cumt - jnp.log(dtt)) * LOG2E

        y_groups = []
        for g in range(SSM_GROUPS):
            ns = slice(g * SSM_STATE, (g + 1) * SSM_STATE)
            gs = slice(g * SSM_GROUP_COLS, (g + 1) * SSM_GROUP_COLS)
            bg = b_ref[0, rows, ns]
            cg = c_ref[0, rows, ns]
            cb = _dot(cg, bg, _NT)
            st = state[g]
            y_off = _dot(cg, st.astype(BF16)) * ec_exp[:, gs]
            diag = []
            for pp in range(SSM_HEADS_PER_GROUP // 2):
                h0 = g * SSM_HEADS_PER_GROUP + 2 * pp
                lms = []
                for a in range(2):
                    col = off + h0 + a
                    diff = col_e[:, col:col + 1] - row_e[col:col + 1, :]
                    lmat = jnp.where(mask, jnp.exp2(diff) * cb, 0.0)
                    lms.append(lmat.astype(BF16))
                xp = xs[:, h0 * SSM_HEAD_DIM:(h0 + 2) * SSM_HEAD_DIM]
                zero = jnp.zeros_like(xp)
                rhs = jnp.concatenate([jnp.where(lo_half, xp, zero),
                                       jnp.where(lo_half, zero, xp)], axis=0)
                diag.append(_dot(jnp.concatenate(lms, axis=1), rhs))
            y_groups.append(jnp.concatenate(diag, axis=1) + y_off)
            state[g] = ec_exp[far:far + 1, gs] * st + _dot(bg, xw[:, gs], _TN)
        y = jnp.concatenate(y_groups, axis=1)

        if reverse:
            y = y + yf_ref[0, rows, :].astype(F32) + d_ref[...] * xs_f
            y = y * _silu(z_ref[0, rows, :].astype(F32))
            ms = jnp.mean(y * y, axis=-1, keepdims=True)
            y = y * lax.rsqrt(ms + RMS_EPS) * nw_ref[...]
        y_ref[0, rows, :] = y.astype(y_ref.dtype)
        return carry

    lax.fori_loop(0, cps, chunk, 0, unroll=True)


def _ssd_scan(xbc, dt, cum, dtt, cumt, reverse, cps, extra=None):
    bsz, s, _ = xbc.shape
    nblk = s // (cps * SSM_CHUNK)
    off = SSM_HEADS if reverse else 0
    expand = np.zeros((2, LANES, SSM_INNER), np.float32)
    for h in range(SSM_HEADS):
        expand[:, off + h, h * SSM_HEAD_DIM:(h + 1) * SSM_HEAD_DIM] = 1.0
    expand = jnp.asarray(expand.reshape(2 * LANES, SSM_INNER), BF16)
    cidx = (lambda c: nblk - 1 - c) if reverse else (lambda c: c)
    b_blk = SSM_INNER // (SSM_GROUPS * SSM_STATE)
    tok = lambda w, blk=0: pl.BlockSpec((1, cps * SSM_CHUNK, w), lambda b, c: (b, cidx(c), blk))
    tr = pl.BlockSpec((1, cps, LANES, SSM_CHUNK), lambda b, c: (b, cidx(c), 0, 0))
    in_specs = [
        tok(SSM_INNER), tok(SSM_GROUPS * SSM_STATE, b_blk), tok(SSM_GROUPS * SSM_STATE, b_blk + 1),
        tok(LANES), tok(LANES), tr, tr,
        pl.BlockSpec((2 * LANES, SSM_INNER), lambda b, c: (0, 0)),
    ]
    args = [xbc, xbc, xbc, dt, cum, dtt, cumt, expand]
    if reverse:
        yf, zx, d_exp, norm_w = extra
        row = pl.BlockSpec((1, SSM_INNER), lambda b, c: (0, 0))
        in_specs += [tok(SSM_INNER), tok(SSM_INNER), row, row]
        args += [yf, zx, d_exp, norm_w]
    return pl.pallas_call(
        functools.partial(_ssd_kernel, reverse=reverse, cps=cps),
        grid=(bsz, nblk),
        in_specs=in_specs,
        out_specs=tok(SSM_INNER),
        out_shape=jax.ShapeDtypeStruct((bsz, s, SSM_INNER), BF16),
        scratch_shapes=[pltpu.VMEM((SSM_GROUPS, SSM_STATE, SSM_GROUP_COLS), F32)],
        compiler_params=_params("parallel", "arbitrary"),
        name="ssd_bwd" if reverse else "ssd_fwd",
    )(*args)


def _out_kernel(y_ref, x_ref, w_ref, gm_ref, lg_ref, lb_ref, out_ref, *, parts):
    rows_per = y_ref.shape[1] // parts
    for part in range(parts):
        rows = slice(part * rows_per, (part + 1) * rows_per)
        out = _dot(y_ref[0, rows, :], w_ref[...])
        res = DEEPNORM_ALPHA * x_ref[0, rows, :] + gm_ref[0] * out
        out_ref[0, rows, :] = _layer_norm_rows(res, lg_ref[...], lb_ref[...])


def _out_proj(y, x, w_out, gate_mod, ln_g, ln_b, tm):
    bsz, s, k = y.shape
    return pl.pallas_call(
        functools.partial(_out_kernel, parts=tm // 512),
        grid=(bsz, s // tm),
        in_specs=[
            pl.BlockSpec((1, tm, k), lambda b, i: (b, i, 0)),
            pl.BlockSpec((1, tm, D_MODEL), lambda b, i: (b, i, 0)),
            pl.BlockSpec((None, k, D_MODEL), lambda b, i: (w_out[1], 0, 0)),
            pl.BlockSpec((1, 1, D_MODEL), lambda b, i: (b, 0, 0)),
            pl.BlockSpec((1, D_MODEL), lambda b, i: (0, 0)),
            pl.BlockSpec((1, D_MODEL), lambda b, i: (0, 0)),
        ],
        out_specs=pl.BlockSpec((1, tm, D_MODEL), lambda b, i: (b, i, 0)),
        out_shape=jax.ShapeDtypeStruct((bsz, s, D_MODEL), F32),
        compiler_params=_params("parallel", "parallel"),
        name="out_proj_ln",
    )(y, x, w_out[0], gate_mod, ln_g, ln_b)


def _layer_a(x, shift, scale, gate_mod, w_in, w_out, ln_g, ln_b):
    gw = 3 * A_WIDTH
    q_scale = LOG2E / math.sqrt(A_HEAD_DIM)
    os_, ls_ = [], []
    for g, (_, dil) in enumerate(DILATION_PAIRS):
        blocks = [(gw, g)]
        if g == 0:
            blocks.append((A_WIDTH, N_DIL * gw // A_WIDTH))
        qkv, *rest = _inproj(x, shift, scale, w_in, blocks, tm=1024, tn=A_WIDTH, dil=dil,
                             lead_scale=q_scale)
        if g == 0:
            gate = rest[0]
        o, lse = _dilated_attention(qkv, tq=1024)
        os_.append(o)
        ls_.append(lse)
    return _amerge_out(os_, ls_, gate, x, w_out, gate_mod, ln_g, ln_b, tm=512)


def _layer_b(x, shift, scale, gate_mod, w_in, conv_w, conv_b, dt_bias, a_log, d_skip,
             norm_w, w_out, ln_g, ln_b):
    n_main = SSM_INNER + SSM_CONV_DIM
    pad = lambda v: jnp.zeros((1, LANES), F32).at[0, :2 * SSM_HEADS].set(v.reshape(-1))
    zx, dt, cum, dtt, cumt = _inproj(x, shift, scale, w_in, [(n_main, 0)], tm=512, tn=1024,
                                     dt_params=(n_main // LANES, pad(dt_bias), pad(a_log)))
    zx = zx[:, 0]
    xbc = _conv_silu(zx, conv_w, conv_b, tm=2048)
    yf = _ssd_scan(xbc, dt, cum, dtt, cumt, reverse=False, cps=4)
    d_exp = jnp.repeat(d_skip.astype(F32), SSM_HEAD_DIM).reshape(1, SSM_INNER)
    yn = _ssd_scan(xbc, dt, cum, dtt, cumt, reverse=True, cps=4,
                   extra=(yf, zx, d_exp, norm_w.reshape(1, SSM_INNER).astype(F32)))
    return _out_proj(yn, x, w_out, gate_mod, ln_g, ln_b, tm=1024)


def kernel(x, c, ada_w, ada_b, ln_g, ln_b, a_w_in, a_w_out, b_w_in, b_conv_w, b_conv_b,
           b_dt_bias, b_a_log, b_d, b_norm_w, b_w_out):
    bsz = x.shape[0]
    mod = _modulation(c, ada_w, ada_b)
    a_w_in, a_w_out, b_w_out = (w.astype(BF16) for w in (a_w_in, a_w_out, b_w_out))
    lane_pad = -b_w_in.shape[2] % LANES
    b_w_in = jnp.pad(b_w_in, ((0, 0), (0, 0), (0, lane_pad))).astype(BF16)
    for i in range(DEPTH):
        m3 = mod[i].reshape(bsz, 3, 1, D_MODEL)
        shift, scale, gate_mod = m3[:, 0], m3[:, 1], m3[:, 2]
        lg = ln_g[i].reshape(1, D_MODEL)
        lb = ln_b[i].reshape(1, D_MODEL)
        j = i // 2
        if i % 2 == 0:
            x = _layer_a(x, shift, scale, gate_mod, (a_w_in, j), (a_w_out, j), lg, lb)
        else:
            x = _layer_b(x, shift, scale, gate_mod, (b_w_in, j), b_conv_w[j], b_conv_b[j],
                         b_dt_bias[j], b_a_log[j], b_d[j], b_norm_w[j], (b_w_out, j), lg, lb)
    return x
```
